```python
import jax, jax.numpy as jnp
from jax import lax
import numpy as np

D_MODEL = 4096
BATCH = 2
SEQ = 8192
DEPTH = 2

HEAD_DIM = 128
BRANCH_WIDTH = D_MODEL // 2
RET_HEADS = D_MODEL // 512
RET_DK = HEAD_DIM
RET_DV = 2 * HEAD_DIM
RET_CHUNK = 128
RET_THETA = 10000.0
SWA_Q_HEADS = D_MODEL // 256
SWA_KV_HEADS = SWA_Q_HEADS // 4
SWA_WINDOW = 128
SWA_BLOCK = 128
ROPE_THETA = 500000.0
ROPE_DIM = HEAD_DIM // 4
LRU_WIDTH = D_MODEL // 2
LRU_BLOCKS = 16
LRU_BLOCK_DIM = LRU_WIDTH // LRU_BLOCKS
CONV_WIDTH = 4
LRU_C = 8.0
N_BRANCH = 3
D_FF = 4 * D_MODEL
EPS = 1e-6

IN_SIZES = (
    RET_HEADS * RET_DK,
    RET_HEADS * RET_DK,
    RET_HEADS * RET_DV,
    RET_HEADS * RET_DV,
    SWA_Q_HEADS * HEAD_DIM,
    SWA_KV_HEADS * HEAD_DIM,
    SWA_KV_HEADS * HEAD_DIM,
    LRU_WIDTH,
    LRU_WIDTH,
    N_BRANCH * D_MODEL,
)
IN_TOTAL = 25600

kernel_name = "hybrid_retention_swa_rglru_block"

F32 = jnp.float32


def _rms(x):
    xf = x.astype(F32)
    return xf * lax.rsqrt(jnp.mean(xf * xf, axis=-1, keepdims=True) + EPS)


def rmsnorm(x, gain):
    return (_rms(x) * gain.astype(F32)).astype(x.dtype)


def rope(x, positions, rot_dim, theta):
    half = rot_dim // 2
    freqs = theta ** (-jnp.arange(half, dtype=F32) / half)
    ang = positions.astype(F32)[:, :, None, None] * freqs
    cos, sin = jnp.cos(ang), jnp.sin(ang)
    xf = x.astype(F32)
    x1, x2, rest = xf[..., :half], xf[..., half:rot_dim], xf[..., rot_dim:]
    out = jnp.concatenate([x1 * cos - x2 * sin, x2 * cos + x1 * sin, rest], axis=-1)
    return out.astype(x.dtype)


def retention(q, k, v, g, positions):
    B, S, H, dk = q.shape
    dv = v.shape[-1]
    C = RET_CHUNK
    N = S // C
    qf = rope(q, positions, dk, RET_THETA).astype(F32)
    kf = rope(k, positions, dk, RET_THETA).astype(F32) * (dk ** -0.5)
    vf = v.astype(F32)
    log_g = jnp.log1p(-jnp.exp2(-5.0 - jnp.arange(H, dtype=F32)))
    idx = jnp.arange(C, dtype=F32)
    diff = idx[:, None] - idx[None, :]
    causal = diff >= 0
    decay_in = jnp.where(causal[None], jnp.exp(log_g[:, None, None] * jnp.where(causal, diff, 0.0)[None]), 0.0)
    xi = jnp.exp(log_g[:, None] * (idx + 1.0))
    zeta = jnp.exp(log_g[:, None] * (C - 1.0 - idx))
    g_chunk = jnp.exp(log_g * C)
    qc = qf.reshape(B, N, C, H, dk)
    kc = kf.reshape(B, N, C, H, dk)
    vc = vf.reshape(B, N, C, H, dv)
    scores = jnp.einsum('bnihd,bnjhd->bnhij', qc, kc) * decay_in
    intra = jnp.einsum('bnhij,bnjhv->bnihv', scores, vc)
    kv = jnp.einsum('bnjhd,bnjhv,hj->nbhdv', kc, vc, zeta)

    def step(state, kv_n):
        return g_chunk[None, :, None, None] * state + kv_n, state

    _, prev = lax.scan(step, jnp.zeros((B, H, dk, dv), F32), kv)
    cross = jnp.einsum('bnihd,hi,nbhdv->bnihv', qc, xi, prev)
    y = (intra + cross).reshape(B, S, H, dv)
    y = _rms(y)
    out = jax.nn.silu(g.astype(F32)) * y
    return out.reshape(B, S, H * dv).astype(q.dtype)


def swa_attention(q, k, v, q_gain, k_gain, sinks, positions):
    B, S, Hq, D = q.shape
    Hkv = k.shape[2]
    G = Hq // Hkv
    C = SWA_BLOCK
    N = S // C
    q = rope(rmsnorm(q, q_gain), positions, ROPE_DIM, ROPE_THETA)
    k = rope(rmsnorm(k, k_gain), positions, ROPE_DIM, ROPE_THETA)
    qb = q.reshape(B, N, C, Hkv, G, D)
    kb = k.reshape(B, N, C, Hkv, D)
    vb = v.reshape(B, N, C, Hkv, D)
    pad = jnp.zeros_like(kb[:, :1])
    k2 = jnp.concatenate([jnp.concatenate([pad, kb[:, :-1]], axis=1), kb], axis=2)
    v2 = jnp.concatenate([jnp.concatenate([pad, vb[:, :-1]], axis=1), vb], axis=2)
    s = jnp.einsum('bnqkgd,bnskd->bnkgqs', qb, k2).astype(F32) * (D ** -0.5)
    qi = jnp.arange(C)[:, None]
    sj = jnp.arange(2 * C)[None, :]
    rel = C + qi - sj
    band = (rel >= 0) & (rel < SWA_WINDOW)
    valid = band[None] & ((jnp.arange(N)[:, None, None] > 0) | (sj >= C)[None])
    s = jnp.where(valid[None, :, None, None], s, -jnp.inf)
    sink = sinks.astype(F32).reshape(Hkv, G)[None, None, :, :, None, None]
    m = jnp.maximum(jnp.max(s, axis=-1, keepdims=True), sink)
    p = jnp.exp(s - m)
    p = p / (jnp.sum(p, axis=-1, keepdims=True) + jnp.exp(sink - m))
    o = jnp.einsum('bnkgqs,bnskd->bnqkgd', p.astype(v.dtype), v2)
    return o.reshape(B, S, Hq * D)


def rglru_branch(xb, yb, conv_w, conv_b, wa, ba, wx, bx, lam):
    B, S, W = xb.shape
    xc = lax.conv_general_dilated(
        xb, conv_w[:, None, :].astype(xb.dtype), window_strides=(1,),
        padding=[(CONV_WIDTH - 1, 0)], dimension_numbers=('NWC', 'WIO', 'NWC'),
        feature_group_count=W) + conv_b.astype(xb.dtype)
    xh = xc.reshape(B, S, LRU_BLOCKS, LRU_BLOCK_DIM)
    r = jax.nn.sigmoid(jnp.einsum('bshi,hij->bshj', xh, wa).reshape(B, S, W).astype(F32) + ba.astype(F32))
    i = jax.nn.sigmoid(jnp.einsum('bshi,hij->bshj', xh, wx).reshape(B, S, W).astype(F32) + bx.astype(F32))
    log_a = -LRU_C * r * jax.nn.softplus(-lam.astype(F32))
    a = jnp.exp(log_a)
    u = jnp.sqrt(-jnp.expm1(2.0 * log_a)) * i * xc.astype(F32)

    def combine(left, right):
        a1, b1 = left
        a2, b2 = right
        return a1 * a2, a2 * b1 + b2

    _, h = lax.associative_scan(combine, (a, u), axis=1)
    return (h * jax.nn.gelu(yb.astype(F32))).astype(xb.dtype)


def hybrid_mixer(h, positions, w_in, swa_q_gain, swa_k_gain, swa_sinks, conv_w, conv_b,
                 lru_wa, lru_ba, lru_wx, lru_bx, lru_lambda, w_branch, w_out):
    B, S, _ = h.shape
    z = jnp.einsum('bsd,de->bse', h, w_in)
    split_points = np.cumsum(np.array(IN_SIZES))[:-1].tolist()
    rq, rk, rv, rg, sq, sk, sv, lx, ly, gates = jnp.split(z, split_points, axis=-1)
    o_ret = retention(rq.reshape(B, S, RET_HEADS, RET_DK), rk.reshape(B, S, RET_HEADS, RET_DK),
                      rv.reshape(B, S, RET_HEADS, RET_DV), rg.reshape(B, S, RET_HEADS, RET_DV), positions)
    o_swa = swa_attention(sq.reshape(B, S, SWA_Q_HEADS, HEAD_DIM), sk.reshape(B, S, SWA_KV_HEADS, HEAD_DIM),
                          sv.reshape(B, S, SWA_KV_HEADS, HEAD_DIM), swa_q_gain, swa_k_gain, swa_sinks, positions)
    o_lru = rglru_branch(lx, ly, conv_w, conv_b, lru_wa, lru_ba, lru_wx, lru_bx, lru_lambda)
    branches = jnp.stack([o_ret, o_swa, o_lru], axis=2)
    proj = jnp.einsum('bskw,kwd->bskd', branches, w_branch)
    gate = jax.nn.sigmoid(gates.reshape(B, S, N_BRANCH, D_MODEL))
    mixed = jnp.sum(gate * proj, axis=2)
    return jnp.einsum('bsd,de->bse', mixed, w_out)


def setup_inputs(seed: int = 0) -> dict:
    key = jax.random.key(seed)
    ks = jax.random.split(key, 20)
    L = DEPTH
    nrm = jax.random.normal
    x = nrm(ks[0], (BATCH, SEQ, D_MODEL), F32)
    offsets = jax.random.randint(ks[1], (BATCH, 1), 0, 4096, dtype=jnp.int32)
    positions = offsets + jnp.arange(SEQ, dtype=jnp.int32)[None, :]
    norm_mix = 1.0 + 0.01 * nrm(ks[2], (L, D_MODEL), F32)
    w_in = nrm(ks[3], (L, D_MODEL, IN_TOTAL), F32) * D_MODEL ** -0.5
    swa_q_gain = 1.0 + 0.01 * nrm(ks[4], (L, HEAD_DIM), F32)
    swa_k_gain = 1.0 + 0.01 * nrm(ks[5], (L, HEAD_DIM), F32)
    swa_sinks = 0.5 * nrm(ks[6], (L, SWA_Q_HEADS), F32)
    conv_w = nrm(ks[7], (L, CONV_WIDTH, LRU_WIDTH), F32) * CONV_WIDTH ** -0.5
    conv_b = 0.01 * nrm(ks[8], (L, LRU_WIDTH), F32)
    lru_wa = nrm(ks[9], (L, LRU_BLOCKS, LRU_BLOCK_DIM, LRU_BLOCK_DIM), F32) * LRU_BLOCK_DIM ** -0.5
    lru_ba = 0.01 * nrm(ks[10], (L, LRU_WIDTH), F32)
    lru_wx = nrm(ks[11], (L, LRU_BLOCKS, LRU_BLOCK_DIM, LRU_BLOCK_DIM), F32) * LRU_BLOCK_DIM ** -0.5
    lru_bx = 0.01 * nrm(ks[12], (L, LRU_WIDTH), F32)
    a_c = jax.random.uniform(ks[13], (L, LRU_WIDTH), F32, 0.9, 0.999)
    a0 = a_c ** (1.0 / LRU_C)
    lru_lambda = jnp.log(a0) - jnp.log1p(-a0)
    w_branch = nrm(ks[14], (L, N_BRANCH, BRANCH_WIDTH, D_MODEL), F32) * BRANCH_WIDTH ** -0.5
    w_out = nrm(ks[15], (L, D_MODEL, D_MODEL), F32) * D_MODEL ** -0.5
    norm_mlp = 1.0 + 0.01 * nrm(ks[16], (L, D_MODEL), F32)
    w_mlp_in = nrm(ks[17], (L, D_MODEL, D_FF), F32) * D_MODEL ** -0.5
    w_mlp_out = nrm(ks[18], (L, D_FF, D_MODEL), F32) * D_FF ** -0.5
    return {"x": x, "positions": positions, "norm_mix": norm_mix, "w_in": w_in,
            "swa_q_gain": swa_q_gain, "swa_k_gain": swa_k_gain, "swa_sinks": swa_sinks,
            "conv_w": conv_w, "conv_b": conv_b, "lru_wa": lru_wa, "lru_ba": lru_ba,
            "lru_wx": lru_wx, "lru_bx": lru_bx, "lru_lambda": lru_lambda,
            "w_branch": w_branch, "w_out": w_out, "norm_mlp": norm_mlp,
            "w_mlp_in": w_mlp_in, "w_mlp_out": w_mlp_out}


def reference(x, positions, norm_mix, w_in, swa_q_gain, swa_k_gain, swa_sinks, conv_w, conv_b,
              lru_wa, lru_ba, lru_wx, lru_bx, lru_lambda, w_branch, w_out, norm_mlp,
              w_mlp_in, w_mlp_out):
    for l in range(DEPTH):
        h = rmsnorm(x, norm_mix[l])
        x = x + hybrid_mixer(h, positions, w_in[l], swa_q_gain[l], swa_k_gain[l], swa_sinks[l],
                             conv_w[l], conv_b[l], lru_wa[l], lru_ba[l], lru_wx[l], lru_bx[l],
                             lru_lambda[l], w_branch[l], w_out[l])
        h = rmsnorm(x, norm_mlp[l])
        u = jnp.einsum('bsd,df->bsf', h, w_mlp_in[l])
        x = x + jnp.einsum('bsf,fd->bsd', jnp.square(jax.nn.relu(u)), w_mlp_out[l])
    return x
```

```python
import functools

import jax
import jax.numpy as jnp
from jax import lax
from jax.experimental import pallas as pl
from jax.experimental.pallas import tpu as pltpu

F32 = jnp.float32
BF16 = jnp.bfloat16

D_MODEL = 4096
DEPTH = 2
HEAD_DIM = 128
RET_HEADS = D_MODEL // 512
RET_DK = HEAD_DIM
RET_DV = 2 * HEAD_DIM
RET_THETA = 10000.0
SWA_Q_HEADS = D_MODEL // 256
SWA_KV_HEADS = SWA_Q_HEADS // 4
SWA_GROUP = SWA_Q_HEADS // SWA_KV_HEADS
SWA_WINDOW = 128
ROPE_THETA = 500000.0
ROPE_DIM = HEAD_DIM // 4
LRU_WIDTH = D_MODEL // 2
LRU_BLOCKS = 16
LRU_BLOCK_DIM = LRU_WIDTH // LRU_BLOCKS
CONV_WIDTH = 4
LRU_C = 8.0
N_BRANCH = 3
BRANCH_WIDTH = D_MODEL // 2
D_FF = 4 * D_MODEL
EPS = 1e-6

OFF_RQ = 0
OFF_RK = OFF_RQ + RET_HEADS * RET_DK
OFF_RV = OFF_RK + RET_HEADS * RET_DK
OFF_RG = OFF_RV + RET_HEADS * RET_DV
OFF_SQ = OFF_RG + RET_HEADS * RET_DV
OFF_SK = OFF_SQ + SWA_Q_HEADS * HEAD_DIM
OFF_SV = OFF_SK + SWA_KV_HEADS * HEAD_DIM
OFF_LX = OFF_SV + SWA_KV_HEADS * HEAD_DIM
OFF_LY = OFF_LX + LRU_WIDTH
OFF_GATE = OFF_LY + LRU_WIDTH
IN_TOTAL = OFF_GATE + N_BRANCH * D_MODEL

LANES = 128
VMEM_LIMIT = 56 * 1024 * 1024


def _params(semantics, vmem=VMEM_LIMIT):
    return pltpu.CompilerParams(dimension_semantics=semantics, vmem_limit_bytes=vmem)


def _rope_tables_kernel(pos_ref, fr_ref, sr_ref, fs_ref, ma_ref, mb_ref,
                        rc_ref, rs_ref, sc_ref, sa_ref, sb_ref):
    pos = pos_ref[...].astype(F32)
    ang_r = pos * fr_ref[...]
    rc_ref[...] = jnp.cos(ang_r)
    rs_ref[...] = jnp.sin(ang_r) * sr_ref[...]
    ang_s = pos * fs_ref[...]
    sn = jnp.sin(ang_s)
    sc_ref[...] = jnp.cos(ang_s)
    sa_ref[...] = sn * ma_ref[...]
    sb_ref[...] = sn * mb_ref[...]


def _rope_tables(pos):
    m = pos.shape[0]
    t = 1024
    half_r = RET_DK // 2
    f_r = RET_THETA ** (-jnp.arange(half_r, dtype=F32) / half_r)
    fr = jnp.concatenate([f_r, f_r])[None, :]
    sr = jnp.concatenate([-jnp.ones((half_r,), F32), jnp.ones((half_r,), F32)])[None, :]
    half_s = ROPE_DIM // 2
    f_s = ROPE_THETA ** (-jnp.arange(half_s, dtype=F32) / half_s)
    zeros_rest = jnp.zeros((HEAD_DIM - ROPE_DIM,), F32)
    fs = jnp.concatenate([f_s, f_s, zeros_rest])[None, :]
    ma = jnp.concatenate([-jnp.ones((half_s,), F32), jnp.zeros((half_s,), F32), zeros_rest])[None, :]
    mb = jnp.concatenate([jnp.zeros((half_s,), F32), jnp.ones((half_s,), F32), zeros_rest])[None, :]
    row = pl.BlockSpec((1, LANES), lambda i: (0, 0))
    tab = pl.BlockSpec((t, LANES), lambda i: (i, 0))
    return pl.pallas_call(
        _rope_tables_kernel,
        grid=(m // t,),
        in_specs=[pl.BlockSpec((t, 1), lambda i: (i, 0)), row, row, row, row, row],
        out_specs=[tab] * 5,
        out_shape=[jax.ShapeDtypeStruct((m, LANES), F32)] * 5,
        compiler_params=_params(("parallel",)),
        name="rope_tables",
    )(pos, fr, sr, fs, ma, mb)


def _rmsnorm_kernel(x_ref, g_ref, o_ref):
    x = x_ref[...]
    ms = jnp.mean(x * x, axis=-1, keepdims=True)
    o_ref[...] = (x * lax.rsqrt(ms + EPS) * g_ref[...]).astype(o_ref.dtype)


def _rmsnorm(x, gain):
    m, d = x.shape
    t = 256
    return pl.pallas_call(
        _rmsnorm_kernel,
        grid=(m // t,),
        in_specs=[pl.BlockSpec((t, d), lambda i: (i, 0)),
                  pl.BlockSpec((1, d), lambda i: (0, 0))],
        out_specs=pl.BlockSpec((t, d), lambda i: (i, 0)),
        out_shape=jax.ShapeDtypeStruct((m, d), BF16),
        compiler_params=_params(("parallel",)),
        name="rmsnorm",
    )(x, gain[None, :])


def _mm_kernel(a_ref, b_ref, *rest, nk, epilogue):
    d = jnp.dot(a_ref[...], b_ref[...], preferred_element_type=F32)
    if epilogue == "residual":
        r_ref, o_ref = rest
        if nk == 1:
            o_ref[...] = r_ref[...] + d
        else:
            k = pl.program_id(2)

            @pl.when(k == 0)
            def _():
                o_ref[...] = r_ref[...] + d

            @pl.when(k > 0)
            def _():
                o_ref[...] += d
    else:
        (o_ref,) = rest
        if epilogue == "relu2":
            d = jnp.maximum(d, 0.0)
            d = d * d
        o_ref[...] = d.astype(o_ref.dtype)


def _matmul(a, b, *, epilogue="cast", residual=None, tm=1024, tn=1024, tk=None, name="matmul"):
    m, kdim = a.shape
    n = b.shape[1]
    tk = kdim if tk is None else tk
    nk = kdim // tk
    assert m % tm == 0 and n % tn == 0 and kdim % tk == 0
    assert nk == 1 or epilogue == "residual"
    in_specs = [pl.BlockSpec((tm, tk), lambda i, j, k: (i, k)),
                pl.BlockSpec((tk, tn), lambda i, j, k: (k, j))]
    args = [a, b]
    if epilogue == "residual":
        in_specs.append(pl.BlockSpec((tm, tn), lambda i, j, k: (i, j)))
        args.append(residual)
        out_dtype = F32
    else:
        out_dtype = BF16
    return pl.pallas_call(
        functools.partial(_mm_kernel, nk=nk, epilogue=epilogue),
        grid=(m // tm, n // tn, nk),
        in_specs=in_specs,
        out_specs=pl.BlockSpec((tm, tn), lambda i, j, k: (i, j)),
        out_shape=jax.ShapeDtypeStruct((m, n), out_dtype),
        compiler_params=_params(("parallel", "parallel", "arbitrary")),
        name=name,
    )(*args)


def _ret_kernel(q_ref, k_ref, v_ref, g_ref, rc_ref, rs_ref, lg_ref, o_ref,
                state_ref, dmat_ref, *, t):
    n = pl.program_id(2)
    lg = lg_ref[0][:, :1]
    ii = lax.broadcasted_iota(jnp.int32, (t, 1), 0).astype(F32)

    @pl.when(n == 0)
    def _():
        state_ref[...] = jnp.zeros_like(state_ref)
        r = lax.broadcasted_iota(jnp.int32, (t, t), 0)
        c = lax.broadcasted_iota(jnp.int32, (t, t), 1)
        diff = (r - c).astype(F32)
        dmat_ref[...] = jnp.where(diff >= 0.0, jnp.exp(lg * jnp.maximum(diff, 0.0)), 0.0)

    rc = rc_ref[...]
    rs = rs_ref[...]
    q = q_ref[...].astype(F32)
    k = k_ref[...].astype(F32)
    qr = q * rc + pltpu.roll(q, RET_DK // 2, 1) * rs
    kr = (k * rc + pltpu.roll(k, RET_DK // 2, 1) * rs) * (RET_DK ** -0.5)
    v = v_ref[...]
    s = lax.dot_general(qr.astype(BF16), kr.astype(BF16), (((1,), (1,)), ((), ())),
                        preferred_element_type=F32)
    p = (s * dmat_ref[...]).astype(BF16)
    intra = jnp.dot(p, v, preferred_element_type=F32)
    xi = jnp.exp(lg * (ii + 1.0))
    st = state_ref[...]
    cross = jnp.dot((qr * xi).astype(BF16), st.astype(BF16), preferred_element_type=F32)
    zeta = jnp.exp(lg * (t - 1.0 - ii))
    kz = (kr * zeta).astype(BF16)
    kv = lax.dot_general(kz, v, (((0,), (0,)), ((), ())), preferred_element_type=F32)
    state_ref[...] = jnp.exp(lg * float(t)) * st + kv
    y = intra + cross
    y = y * lax.rsqrt(jnp.mean(y * y, axis=-1, keepdims=True) + EPS)
    g = g_ref[...].astype(F32)
    o_ref[...] = (g * jax.nn.sigmoid(g) * y).astype(o_ref.dtype)


def _retention(z, rc, rs, batch, seq):
    t = 256
    nb = seq // t
    log_g = jnp.log1p(-jnp.exp2(-5.0 - jnp.arange(RET_HEADS, dtype=F32)))
    lg = jnp.broadcast_to(log_g[:, None, None], (RET_HEADS, 1, LANES))
    kb = OFF_RK // RET_DK
    vb = OFF_RV // RET_DV
    gb = OFF_RG // RET_DV
    return pl.pallas_call(
        functools.partial(_ret_kernel, t=t),
        grid=(batch, RET_HEADS, nb),
        in_specs=[
            pl.BlockSpec((t, RET_DK), lambda b, h, n: (b * nb + n, h)),
            pl.BlockSpec((t, RET_DK), lambda b, h, n: (b * nb + n, kb + h)),
            pl.BlockSpec((t, RET_DV), lambda b, h, n: (b * nb + n, vb + h)),
            pl.BlockSpec((t, RET_DV), lambda b, h, n: (b * nb + n, gb + h)),
            pl.BlockSpec((t, LANES), lambda b, h, n: (b * nb + n, 0)),
            pl.BlockSpec((t, LANES), lambda b, h, n: (b * nb + n, 0)),
            pl.BlockSpec((1, 1, LANES), lambda b, h, n: (h, 0, 0)),
        ],
        out_specs=pl.BlockSpec((t, RET_DV), lambda b, h, n: (b * nb + n, h)),
        out_shape=jax.ShapeDtypeStruct((batch * seq, RET_HEADS * RET_DV), BF16),
        scratch_shapes=[pltpu.VMEM((RET_DK, RET_DV), F32), pltpu.VMEM((t, t), F32)],
        compiler_params=_params(("parallel", "parallel", "arbitrary")),
        name="retention",
    )(z, z, z, z, rc, rs, lg)


def _norm_rope(x, gain, c, sa, sb):
    xf = x.astype(F32)
    xn = xf * lax.rsqrt(jnp.mean(xf * xf, axis=-1, keepdims=True) + EPS) * gain
    half = ROPE_DIM // 2
    return xn * c + pltpu.roll(xn, HEAD_DIM - half, 1) * sa + pltpu.roll(xn, half, 1) * sb


def _swa_kernel(q_ref, kc_ref, kp_ref, vc_ref, vp_ref, c_ref, sa_ref, sb_ref,
                cp_ref, sap_ref, sbp_ref, qg_ref, kg_ref, sink_ref, o_ref, *, tq):
    n = pl.program_id(2)
    w = SWA_WINDOW
    qg = qg_ref[...]
    kg = kg_ref[...]
    c = c_ref[...]
    sa = sa_ref[...]
    sb = sb_ref[...]
    k_prev = _norm_rope(kp_ref[...], kg, cp_ref[...], sap_ref[...], sbp_ref[...])
    k_cur = _norm_rope(kc_ref[...], kg, c, sa, sb)
    k_all = jnp.concatenate([k_prev, k_cur], axis=0).astype(BF16)
    v_all = jnp.concatenate([vp_ref[...], vc_ref[...]], axis=0)
    sink = sink_ref[0][:, :1]
    rows = SWA_GROUP * w
    qi = lax.broadcasted_iota(jnp.int32, (rows, 2 * w), 0) % w
    sj = lax.broadcasted_iota(jnp.int32, (rows, 2 * w), 1)
    band = (sj > qi) & (sj <= qi + w)
    for t in range(tq // w):
        r0 = t * w
        qs = jnp.concatenate(
            [_norm_rope(q_ref[r0:r0 + w, g * HEAD_DIM:(g + 1) * HEAD_DIM], qg,
                        c[r0:r0 + w], sa[r0:r0 + w], sb[r0:r0 + w])
             for g in range(SWA_GROUP)], axis=0).astype(BF16)
        ks = k_all[r0:r0 + 2 * w]
        vs = v_all[r0:r0 + 2 * w]
        s = lax.dot_general(qs, ks, (((1,), (1,)), ((), ())),
                            preferred_element_type=F32) * (HEAD_DIM ** -0.5)
        valid = band & (sj >= jnp.where(n > 0, 0, w)) if t == 0 else band
        s = jnp.where(valid, s, -jnp.inf)
        m = jnp.maximum(jnp.max(s, axis=-1, keepdims=True), sink)
        p = jnp.exp(s - m)
        denom = jnp.sum(p, axis=-1, keepdims=True) + jnp.exp(sink - m)
        o = jnp.dot(p.astype(BF16), vs, preferred_element_type=F32) / denom
        for g in range(SWA_GROUP):
            o_ref[r0:r0 + w, g * HEAD_DIM:(g + 1) * HEAD_DIM] = (
                o[g * w:(g + 1) * w].astype(o_ref.dtype))


def _swa(z, sc, sa, sb, q_gain, k_gain, sinks, batch, seq):
    tq = 256
    w = SWA_WINDOW
    nb = seq // tq
    r = tq // w
    gw = SWA_GROUP * HEAD_DIM
    qb = OFF_SQ // gw
    kb = OFF_SK // HEAD_DIM
    vb = OFF_SV // HEAD_DIM
    sink_b = jnp.broadcast_to(
        jnp.repeat(sinks.astype(F32).reshape(SWA_KV_HEADS, SWA_GROUP), w, axis=1)[:, :, None],
        (SWA_KV_HEADS, SWA_GROUP * w, LANES))

    def cur(b, h, n):
        return b * nb + n

    def prev(b, h, n):
        return b * nb * r + jnp.maximum(n * r - 1, 0)

    return pl.pallas_call(
        functools.partial(_swa_kernel, tq=tq),
        grid=(batch, SWA_KV_HEADS, nb),
        in_specs=[
            pl.BlockSpec((tq, gw), lambda b, h, n: (cur(b, h, n), qb + h)),
            pl.BlockSpec((tq, HEAD_DIM), lambda b, h, n: (cur(b, h, n), kb + h)),
            pl.BlockSpec((w, HEAD_DIM), lambda b, h, n: (prev(b, h, n), kb + h)),
            pl.BlockSpec((tq, HEAD_DIM), lambda b, h, n: (cur(b, h, n), vb + h)),
            pl.BlockSpec((w, HEAD_DIM), lambda b, h, n: (prev(b, h, n), vb + h)),
            pl.BlockSpec((tq, LANES), lambda b, h, n: (cur(b, h, n), 0)),
            pl.BlockSpec((tq, LANES), lambda b, h, n: (cur(b, h, n), 0)),
            pl.BlockSpec((tq, LANES), lambda b, h, n: (cur(b, h, n), 0)),
            pl.BlockSpec((w, LANES), lambda b, h, n: (prev(b, h, n), 0)),
            pl.BlockSpec((w, LANES), lambda b, h, n: (prev(b, h, n), 0)),
            pl.BlockSpec((w, LANES), lambda b, h, n: (prev(b, h, n), 0)),
            pl.BlockSpec((1, HEAD_DIM), lambda b, h, n: (0, 0)),
            pl.BlockSpec((1, HEAD_DIM), lambda b, h, n: (0, 0)),
            pl.BlockSpec((1, SWA_GROUP * w, LANES), lambda b, h, n: (h, 0, 0)),
        ],
        out_specs=pl.BlockSpec((tq, gw), lambda b, h, n: (cur(b, h, n), h)),
        out_shape=jax.ShapeDtypeStruct((batch * seq, SWA_Q_HEADS * HEAD_DIM), BF16),
        compiler_params=_params(("parallel", "parallel", "parallel")),
        name="swa",
    )(z, z, z, z, z, sc, sa, sb, sc, sa, sb, q_gain[None, :], k_gain[None, :], sink_b)


def _lru_kernel(x_ref, y_ref, cw_ref, cb_ref, w_ref, b_ref, lam_ref, o_ref,
                tail_ref, h_ref, *, t):
    n = pl.program_id(2)
    pad = 8

    @pl.when(n == 0)
    def _():
        tail_ref[...] = jnp.zeros_like(tail_ref)
        h_ref[...] = jnp.zeros_like(h_ref)

    x = x_ref[...].astype(F32)
    xcat = jnp.concatenate([tail_ref[...], x], axis=0)
    cw = cw_ref[...]
    xc = x * cw[CONV_WIDTH - 1:CONV_WIDTH] + cb_ref[...]
    for d in range(1, CONV_WIDTH):
        xc = xc + pltpu.roll(xcat, d, 0)[pad:] * cw[CONV_WIDTH - 1 - d:CONV_WIDTH - d]
    tail_ref[...] = x[t - pad:]
    gates = jnp.dot(xc.astype(BF16), w_ref[0], preferred_element_type=F32) + b_ref[0]
    r = jax.nn.sigmoid(gates[:, :LRU_BLOCK_DIM])
    i = jax.nn.sigmoid(gates[:, LRU_BLOCK_DIM:])
    log_a = -LRU_C * r * jax.nn.softplus(-lam_ref[...])
    a = jnp.exp(log_a)
    u = jnp.sqrt(-jnp.tanh(log_a) * (1.0 + a * a)) * i * xc
    row = lax.broadcasted_iota(jnp.int32, (t, 1), 0)
    d = 1
    while d < t:
        keep = row >= d
        a_sh = jnp.where(keep, pltpu.roll(a, d, 0), 1.0)
        u_sh = jnp.where(keep, pltpu.roll(u, d, 0), 0.0)
        u = a * u_sh + u
        a = a * a_sh
        d *= 2
    h = a * h_ref[...] + u
    h_ref[...] = h[t - 1:t]
    o_ref[...] = (h * jax.nn.gelu(y_ref[...].astype(F32))).astype(o_ref.dtype)


def _rglru(z, conv_w, conv_b, wa, ba, wx, bx, lam, batch, seq):
    t = 512
    nb = seq // t
    cwid = LRU_BLOCK_DIM
    xb = OFF_LX // cwid
    yb = OFF_LY // cwid
    w_cat = jnp.concatenate([wa, wx], axis=-1).astype(BF16)
    b_cat = jnp.concatenate([ba.reshape(LRU_BLOCKS, 1, cwid),
                             bx.reshape(LRU_BLOCKS, 1, cwid)], axis=-1)
    return pl.pallas_call(
        functools.partial(_lru_kernel, t=t),
        grid=(batch, LRU_BLOCKS, nb),
        in_specs=[
            pl.BlockSpec((t, cwid), lambda b, c, n: (b * nb + n, xb + c)),
            pl.BlockSpec((t, cwid), lambda b, c, n: (b * nb + n, yb + c)),
            pl.BlockSpec((CONV_WIDTH, cwid), lambda b, c, n: (0, c)),
            pl.BlockSpec((1, cwid), lambda b, c, n: (0, c)),
            pl.BlockSpec((1, cwid, 2 * cwid), lambda b, c, n: (c, 0, 0)),
            pl.BlockSpec((1, 1, 2 * cwid), lambda b, c, n: (c, 0, 0)),
            pl.BlockSpec((1, cwid), lambda b, c, n: (0, c)),
        ],
        out_specs=pl.BlockSpec((t, cwid), lambda b, c, n: (b * nb + n, c)),
        out_shape=jax.ShapeDtypeStruct((batch * seq, LRU_WIDTH), BF16),
        scratch_shapes=[pltpu.VMEM((8, cwid), F32), pltpu.VMEM((1, cwid), F32)],
        compiler_params=_params(("parallel", "parallel", "arbitrary")),
        name="rglru",
    )(z, z, conv_w, conv_b[None, :], w_cat, b_cat, lam[None, :])


def _merge_kernel(a0_ref, a1_ref, a2_ref, w_ref, g0_ref, g1_ref, g2_ref, o_ref):
    acc = None
    for k, (a_ref, g_ref) in enumerate(((a0_ref, g0_ref), (a1_ref, g1_ref), (a2_ref, g2_ref))):
        d = jnp.dot(a_ref[...], w_ref[k], preferred_element_type=F32)
        term = jax.nn.sigmoid(g_ref[...].astype(F32)) * d
        acc = term if acc is None else acc + term
    o_ref[...] = acc.astype(o_ref.dtype)


def _merge(o_ret, o_swa, o_lru, w_branch, z):
    m = o_ret.shape[0]
    tm, tn = 1024, 512
    gb = OFF_GATE // tn
    gstep = D_MODEL // tn
    a_spec = pl.BlockSpec((tm, BRANCH_WIDTH), lambda i, j: (i, 0))

    def gate_spec(k):
        return pl.BlockSpec((tm, tn), lambda i, j: (i, gb + k * gstep + j))

    return pl.pallas_call(
        _merge_kernel,
        grid=(m // tm, D_MODEL // tn),
        in_specs=[a_spec, a_spec, a_spec,
                  pl.BlockSpec((N_BRANCH, BRANCH_WIDTH, tn), lambda i, j: (0, 0, j)),
                  gate_spec(0), gate_spec(1), gate_spec(2)],
        out_specs=pl.BlockSpec((tm, tn), lambda i, j: (i, j)),
        out_shape=jax.ShapeDtypeStruct((m, D_MODEL), BF16),
        compiler_params=_params(("parallel", "parallel")),
        name="merge",
    )(o_ret, o_swa, o_lru, w_branch, z, z, z)


def kernel(x, positions, norm_mix, w_in, swa_q_gain, swa_k_gain, swa_sinks, conv_w, conv_b,
           lru_wa, lru_ba, lru_wx, lru_bx, lru_lambda, w_branch, w_out, norm_mlp,
           w_mlp_in, w_mlp_out):
    batch, seq, d = x.shape
    m = batch * seq
    xf = x.reshape(m, d)
    rc, rs, sc, sa, sb = _rope_tables(positions.reshape(m, 1))
    for l in range(DEPTH):
        h = _rmsnorm(xf, norm_mix[l])
        z = _matmul(h, w_in[l].astype(BF16), name="in_proj")
        o_ret = _retention(z, rc, rs, batch, seq)
        o_swa = _swa(z, sc, sa, sb, swa_q_gain[l], swa_k_gain[l], swa_sinks[l], batch, seq)
        o_lru = _rglru(z, conv_w[l], conv_b[l], lru_wa[l], lru_ba[l], lru_wx[l], lru_bx[l],
                       lru_lambda[l], batch, seq)
        mixed = _merge(o_ret, o_swa, o_lru, w_branch[l].astype(BF16), z)
        xf = _matmul(mixed, w_out[l].astype(BF16), epilogue="residual", residual=xf,
                     name="out_proj")
        h = _rmsnorm(xf, norm_mlp[l])
        u = _matmul(h, w_mlp_in[l].astype(BF16), epilogue="relu2", name="mlp_in")
        xf = _matmul(u, w_mlp_out[l].astype(BF16), epilogue="residual", residual=xf,
                     tk=2048, name="mlp_out")
    return xf.reshape(batch, seq, d)
```

```python
import functools

import jax
import jax.numpy as jnp
from jax import lax
from jax.experimental import pallas as pl
from jax.experimental.pallas import tpu as pltpu

F32 = jnp.float32
BF16 = jnp.bfloat16

D_MODEL = 4096
DEPTH = 2
HEAD_DIM = 128
RET_HEADS = D_MODEL // 512
RET_DK = HEAD_DIM
RET_DV = 2 * HEAD_DIM
RET_THETA = 10000.0
SWA_Q_HEADS = D_MODEL // 256
SWA_KV_HEADS = SWA_Q_HEADS // 4
SWA_GROUP = SWA_Q_HEADS // SWA_KV_HEADS
SWA_WINDOW = 128
ROPE_THETA = 500000.0
ROPE_DIM = HEAD_DIM // 4
LRU_WIDTH = D_MODEL // 2
LRU_BLOCKS = 16
LRU_BLOCK_DIM = LRU_WIDTH // LRU_BLOCKS
CONV_WIDTH = 4
LRU_C = 8.0
N_BRANCH = 3
BRANCH_WIDTH = D_MODEL // 2
D_FF = 4 * D_MODEL
EPS = 1e-6

OFF_RQ = 0
OFF_RK = OFF_RQ + RET_HEADS * RET_DK
OFF_RV = OFF_RK + RET_HEADS * RET_DK
OFF_RG = OFF_RV + RET_HEADS * RET_DV
OFF_SQ = OFF_RG + RET_HEADS * RET_DV
OFF_SK = OFF_SQ + SWA_Q_HEADS * HEAD_DIM
OFF_SV = OFF_SK + SWA_KV_HEADS * HEAD_DIM
OFF_LX = OFF_SV + SWA_KV_HEADS * HEAD_DIM
OFF_LY = OFF_LX + LRU_WIDTH
OFF_GATE = OFF_LY + LRU_WIDTH
IN_TOTAL = OFF_GATE + N_BRANCH * D_MODEL

LANES = 128
SUBLANES = 8
VMEM_LIMIT = 56 * 1024 * 1024


def _params(semantics, vmem=VMEM_LIMIT):
    return pltpu.CompilerParams(dimension_semantics=semantics, vmem_limit_bytes=vmem)


def _rope_tables_kernel(pos_ref, fr_ref, sr_ref, fs_ref, rc_ref, rs_ref, sc_ref, ss_ref):
    pos = pos_ref[...].astype(F32)
    ang_r = pos * fr_ref[...]
    rc_ref[...] = jnp.cos(ang_r)
    rs_ref[...] = jnp.sin(ang_r) * sr_ref[...]
    ang_s = pos * fs_ref[...]
    sc_ref[...] = jnp.cos(ang_s)
    ss_ref[...] = jnp.sin(ang_s)


def _rope_tables(pos):
    m = pos.shape[0]
    t = 1024
    half_r = RET_DK // 2
    f_r = RET_THETA ** (-jnp.arange(half_r, dtype=F32) / half_r)
    fr = jnp.concatenate([f_r, f_r])[None, :]
    sr = jnp.concatenate([-jnp.ones((half_r,), F32), jnp.ones((half_r,), F32)])[None, :]
    half_s = ROPE_DIM // 2
    f_s = ROPE_THETA ** (-jnp.arange(half_s, dtype=F32) / half_s)
    zeros_rest = jnp.zeros((HEAD_DIM - ROPE_DIM,), F32)
    fs = jnp.concatenate([f_s, f_s, zeros_rest])[None, :]
    row = pl.BlockSpec((1, LANES), lambda i: (0, 0))
    tab = pl.BlockSpec((t, LANES), lambda i: (i, 0))
    return pl.pallas_call(
        _rope_tables_kernel,
        grid=(m // t,),
        in_specs=[pl.BlockSpec((t, 1), lambda i: (i, 0)), row, row, row],
        out_specs=[tab] * 4,
        out_shape=[jax.ShapeDtypeStruct((m, LANES), F32)] * 4,
        compiler_params=_params(("parallel",)),
        name="rope_tables",
    )(pos, fr, sr, fs)


def _rmsnorm_kernel(x_ref, g_ref, o_ref):
    x = x_ref[...]
    ms = jnp.mean(x * x, axis=-1, keepdims=True)
    o_ref[...] = (x * lax.rsqrt(ms + EPS) * g_ref[...]).astype(o_ref.dtype)


def _rmsnorm(x, gain):
    m, d = x.shape
    t = 256
    return pl.pallas_call(
        _rmsnorm_kernel,
        grid=(m // t,),
        in_specs=[pl.BlockSpec((t, d), lambda i: (i, 0)),
                  pl.BlockSpec((1, d), lambda i: (0, 0))],
        out_specs=pl.BlockSpec((t, d), lambda i: (i, 0)),
        out_shape=jax.ShapeDtypeStruct((m, d), BF16),
        compiler_params=_params(("parallel",)),
        name="rmsnorm",
    )(x, gain[None, :])


def _mm_kernel(a_ref, b_ref, *rest, nk, epilogue):
    d = jnp.dot(a_ref[...], b_ref[...], preferred_element_type=F32)
    if epilogue == "residual":
        r_ref, o_ref = rest
        if nk == 1:
            o_ref[...] = r_ref[...] + d
        else:
            k = pl.program_id(2)

            @pl.when(k == 0)
            def _():
                o_ref[...] = r_ref[...] + d

            @pl.when(k > 0)
            def _():
                o_ref[...] += d
    else:
        (o_ref,) = rest
        if epilogue == "relu2":
            d = jnp.maximum(d, 0.0)
            d = d * d
        o_ref[...] = d.astype(o_ref.dtype)


def _matmul(a, b, layer, *, epilogue="cast", residual=None, tm=1024, tn=1024, tk=None,
            name="matmul"):
    m, kdim = a.shape
    n = b.shape[2]
    tk = kdim if tk is None else tk
    nk = kdim // tk
    assert m % tm == 0 and n % tn == 0 and kdim % tk == 0
    assert nk == 1 or epilogue == "residual"
    in_specs = [pl.BlockSpec((tm, tk), lambda i, j, k: (i, k)),
                pl.BlockSpec((None, tk, tn), lambda i, j, k: (layer, k, j))]
    args = [a, b]
    if epilogue == "residual":
        in_specs.append(pl.BlockSpec((tm, tn), lambda i, j, k: (i, j)))
        args.append(residual)
        out_dtype = F32
    else:
        out_dtype = BF16
    return pl.pallas_call(
        functools.partial(_mm_kernel, nk=nk, epilogue=epilogue),
        grid=(m // tm, n // tn, nk),
        in_specs=in_specs,
        out_specs=pl.BlockSpec((tm, tn), lambda i, j, k: (i, j)),
        out_shape=jax.ShapeDtypeStruct((m, n), out_dtype),
        compiler_params=_params(("parallel", "parallel", "arbitrary")),
        name=name,
    )(*args)


def _ret_kernel(q_ref, k_ref, v_ref, g_ref, rc_ref, rs_ref, lg_ref, o_ref,
                state_ref, dmat_ref, *, t):
    n = pl.program_id(2)
    lg = lg_ref[0][:, :1]
    ii = lax.broadcasted_iota(jnp.int32, (t, 1), 0).astype(F32)

    @pl.when(n == 0)
    def _():
        state_ref[...] = jnp.zeros_like(state_ref)
        r = lax.broadcasted_iota(jnp.int32, (t, t), 0)
        c = lax.broadcasted_iota(jnp.int32, (t, t), 1)
        diff = (r - c).astype(F32)
        dmat_ref[...] = jnp.where(diff >= 0.0, jnp.exp(lg * jnp.maximum(diff, 0.0)), 0.0)

    rc = rc_ref[...]
    rs = rs_ref[...]
    q = q_ref[...].astype(F32)
    k = k_ref[...].astype(F32)
    qr = q * rc + pltpu.roll(q, RET_DK // 2, 1) * rs
    kr = (k * rc + pltpu.roll(k, RET_DK // 2, 1) * rs) * (RET_DK ** -0.5)
    v = v_ref[...]
    s = lax.dot_general(qr.astype(BF16), kr.astype(BF16), (((1,), (1,)), ((), ())),
                        preferred_element_type=F32)
    p = (s * dmat_ref[...]).astype(BF16)
    intra = jnp.dot(p, v, preferred_element_type=F32)
    xi = jnp.exp(lg * (ii + 1.0))
    st = state_ref[...]
    cross = jnp.dot((qr * xi).astype(BF16), st.astype(BF16), preferred_element_type=F32)
    zeta = jnp.exp(lg * (t - 1.0 - ii))
    kz = (kr * zeta).astype(BF16)
    kv = lax.dot_general(kz, v, (((0,), (0,)), ((), ())), preferred_element_type=F32)
    state_ref[...] = jnp.exp(lg * float(t)) * st + kv
    y = intra + cross
    y = y * lax.rsqrt(jnp.mean(y * y, axis=-1, keepdims=True) + EPS)
    g = g_ref[...].astype(F32)
    o_ref[...] = (g * jax.nn.sigmoid(g) * y).astype(o_ref.dtype)


def _retention(z, rc, rs, batch, seq):
    t = 512
    nb = seq // t
    log_g = jnp.log1p(-jnp.exp2(-5.0 - jnp.arange(RET_HEADS, dtype=F32)))
    lg = jnp.broadcast_to(log_g[:, None, None], (RET_HEADS, 1, LANES))
    kb = OFF_RK // RET_DK
    vb = OFF_RV // RET_DV
    gb = OFF_RG // RET_DV
    return pl.pallas_call(
        functools.partial(_ret_kernel, t=t),
        grid=(batch, RET_HEADS, nb),
        in_specs=[
            pl.BlockSpec((t, RET_DK), lambda b, h, n: (b * nb + n, h)),
            pl.BlockSpec((t, RET_DK), lambda b, h, n: (b * nb + n, kb + h)),
            pl.BlockSpec((t, RET_DV), lambda b, h, n: (b * nb + n, vb + h)),
            pl.BlockSpec((t, RET_DV), lambda b, h, n: (b * nb + n, gb + h)),
            pl.BlockSpec((t, LANES), lambda b, h, n: (b * nb + n, 0)),
            pl.BlockSpec((t, LANES), lambda b, h, n: (b * nb + n, 0)),
            pl.BlockSpec((1, 1, LANES), lambda b, h, n: (h, 0, 0)),
        ],
        out_specs=pl.BlockSpec((t, RET_DV), lambda b, h, n: (b * nb + n, h)),
        out_shape=jax.ShapeDtypeStruct((batch * seq, RET_HEADS * RET_DV), BF16),
        scratch_shapes=[pltpu.VMEM((RET_DK, RET_DV), F32), pltpu.VMEM((t, t), F32)],
        compiler_params=_params(("parallel", "parallel", "arbitrary")),
        name="retention",
    )(z, z, z, z, rc, rs, lg)


def _norm_rope(x, gain, c, s, ones, rot):
    xf = x.astype(F32)
    ss = jnp.dot((xf * xf).astype(BF16), ones, preferred_element_type=F32)
    xn = xf * lax.rsqrt(ss * (1.0 / HEAD_DIM) + EPS) * gain
    partner = jnp.dot(xn.astype(BF16), rot, preferred_element_type=F32)
    return xn * c + partner * s


def _swa_kernel(q_ref, kc_ref, kp_ref, vc_ref, vp_ref, c_ref, s_ref, cp_ref, sp_ref,
                qg_ref, kg_ref, sink_ref, rot_ref, o_ref, *, tq):
    n = pl.program_id(2)
    w = SWA_WINDOW
    group_rows = SWA_GROUP * w
    ones = jnp.ones((HEAD_DIM, HEAD_DIM), BF16)
    rot = rot_ref[...]
    qg = qg_ref[...] * (HEAD_DIM ** -0.5)
    kg = kg_ref[...]
    c = c_ref[...]
    s = s_ref[...]
    k_prev = _norm_rope(kp_ref[...], kg, cp_ref[...], sp_ref[...], ones, rot)
    k_cur = _norm_rope(kc_ref[...], kg, c, s, ones, rot)
    k_all = jnp.concatenate([k_prev, k_cur], axis=0).astype(BF16)
    v_all = jnp.concatenate([vp_ref[...], vc_ref[...]], axis=0)
    v_ext = jnp.concatenate([v_all, jnp.ones_like(v_all)], axis=1)
    sink = sink_ref[0]
    qi = lax.broadcasted_iota(jnp.int32, (w, 2 * w), 0)
    sj = lax.broadcasted_iota(jnp.int32, (w, 2 * w), 1)
    band = (sj > qi) & (sj <= qi + w)
    bias = jnp.where(band, 0.0, -jnp.inf)
    bias_first = jnp.where(band & (sj >= jnp.where(n > 0, 0, w)), 0.0, -jnp.inf)
    nt = tq // w
    xq = jnp.concatenate([q_ref[t * w:(t + 1) * w, g * HEAD_DIM:(g + 1) * HEAD_DIM]
                          for t in range(nt) for g in range(SWA_GROUP)], axis=0)
    c4 = jnp.concatenate([c[t * w:(t + 1) * w] for t in range(nt)
                          for _ in range(SWA_GROUP)], axis=0)
    s4 = jnp.concatenate([s[t * w:(t + 1) * w] for t in range(nt)
                          for _ in range(SWA_GROUP)], axis=0)
    qs = _norm_rope(xq, qg, c4, s4, ones, rot).astype(BF16)
    sc = jnp.concatenate(
        [(lax.dot_general(qs[t * group_rows:(t + 1) * group_rows], k_all[t * w:(t + 2) * w],
                          (((1,), (1,)), ((), ())), preferred_element_type=F32
                          ).reshape(SWA_GROUP, w, 2 * w)
          + (bias_first if t == 0 else bias)[None]).reshape(group_rows, 2 * w)
         for t in range(nt)], axis=0)
    sink_all = jnp.concatenate([sink] * nt, axis=0)
    m = jnp.maximum(jnp.max(sc, axis=-1, keepdims=True), sink_all[:, :1])
    p = jnp.exp(sc - m).astype(BF16)
    o_ext = jnp.concatenate(
        [jnp.dot(p[t * group_rows:(t + 1) * group_rows], v_ext[t * w:(t + 2) * w],
                 preferred_element_type=F32) for t in range(nt)], axis=0)
    denom = o_ext[:, HEAD_DIM:] + jnp.exp(sink_all - m)
    o = (o_ext[:, :HEAD_DIM] / denom).astype(o_ref.dtype)
    for t in range(nt):
        for g in range(SWA_GROUP):
            r0 = (t * SWA_GROUP + g) * w
            o_ref[t * w:(t + 1) * w, g * HEAD_DIM:(g + 1) * HEAD_DIM] = o[r0:r0 + w]


def _rotate_half_matrix():
    half = ROPE_DIM // 2
    src = jnp.arange(HEAD_DIM)[:, None]
    dst = jnp.arange(HEAD_DIM)[None, :]
    minus = (dst < half) & (src == dst + half)
    plus = (dst >= half) & (dst < ROPE_DIM) & (src == dst - half)
    return (plus.astype(F32) - minus.astype(F32)).astype(BF16)


def _swa(z, sc, ss, q_gain, k_gain, sinks, batch, seq):
    tq = 512
    w = SWA_WINDOW
    nb = seq // tq
    r = tq // w
    gw = SWA_GROUP * HEAD_DIM
    qb = OFF_SQ // gw
    kb = OFF_SK // HEAD_DIM
    vb = OFF_SV // HEAD_DIM
    sink_b = jnp.broadcast_to(
        jnp.repeat(sinks.astype(F32).reshape(SWA_KV_HEADS, SWA_GROUP), w, axis=1)[:, :, None],
        (SWA_KV_HEADS, SWA_GROUP * w, LANES))

    def cur(b, h, n):
        return b * nb + n

    def prev(b, h, n):
        return b * nb * r + jnp.maximum(n * r - 1, 0)

    return pl.pallas_call(
        functools.partial(_swa_kernel, tq=tq),
        grid=(batch, SWA_KV_HEADS, nb),
        in_specs=[
            pl.BlockSpec((tq, gw), lambda b, h, n: (cur(b, h, n), qb + h)),
            pl.BlockSpec((tq, HEAD_DIM), lambda b, h, n: (cur(b, h, n), kb + h)),
            pl.BlockSpec((w, HEAD_DIM), lambda b, h, n: (prev(b, h, n), kb + h)),
            pl.BlockSpec((tq, HEAD_DIM), lambda b, h, n: (cur(b, h, n), vb + h)),
            pl.BlockSpec((w, HEAD_DIM), lambda b, h, n: (prev(b, h, n), vb + h)),
            pl.BlockSpec((tq, LANES), lambda b, h, n: (cur(b, h, n), 0)),
            pl.BlockSpec((tq, LANES), lambda b, h, n: (cur(b, h, n), 0)),
            pl.BlockSpec((w, LANES), lambda b, h, n: (prev(b, h, n), 0)),
            pl.BlockSpec((w, LANES), lambda b, h, n: (prev(b, h, n), 0)),
            pl.BlockSpec((1, HEAD_DIM), lambda b, h, n: (0, 0)),
            pl.BlockSpec((1, HEAD_DIM), lambda b, h, n: (0, 0)),
            pl.BlockSpec((1, SWA_GROUP * w, LANES), lambda b, h, n: (h, 0, 0)),
            pl.BlockSpec((HEAD_DIM, HEAD_DIM), lambda b, h, n: (0, 0)),
        ],
        out_specs=pl.BlockSpec((tq, gw), lambda b, h, n: (cur(b, h, n), h)),
        out_shape=jax.ShapeDtypeStruct((batch * seq, SWA_Q_HEADS * HEAD_DIM), BF16),
        compiler_params=_params(("parallel", "parallel", "parallel")),
        name="swa",
    )(z, z, z, z, z, sc, ss, sc, ss, q_gain[None, :], k_gain[None, :], sink_b,
      _rotate_half_matrix())


def _sigmoid(x):
    return 0.5 * jnp.tanh(0.5 * x) + 0.5


def _lru_kernel(x_ref, y_ref, cw_ref, cb_ref, w_ref, b_ref, lam_ref, o_ref,
                xs_ref, a_ref, u_ref, carry_ref, *, t):
    n = pl.program_id(2)
    pad = SUBLANES
    nseg = SUBLANES
    seg = t // nseg

    @pl.when(n == 0)
    def _():
        xs_ref[0:pad, :] = jnp.zeros((pad, LRU_BLOCK_DIM), F32)
        carry_ref[...] = jnp.zeros_like(carry_ref)

    xs_ref[pad:pad + t, :] = x_ref[...].astype(F32)
    cw = cw_ref[...]
    xc = cb_ref[...] + xs_ref[pad:pad + t, :] * cw[CONV_WIDTH - 1:CONV_WIDTH]
    for d in range(1, CONV_WIDTH):
        xc = xc + xs_ref[pl.ds(pad - d, t), :] * cw[CONV_WIDTH - 1 - d:CONV_WIDTH - d]
    xs_ref[0:pad, :] = xs_ref[t:t + pad, :]

    gates = jnp.dot(xc.astype(BF16), w_ref[0], preferred_element_type=F32) + b_ref[0]
    r = _sigmoid(gates[:, :LRU_BLOCK_DIM])
    i = _sigmoid(gates[:, LRU_BLOCK_DIM:])
    log_a = -LRU_C * r * jax.nn.softplus(-lam_ref[...])
    a = jnp.exp(log_a)
    w = -jnp.tanh(log_a) * (1.0 + a * a)
    u = w * lax.rsqrt(jnp.maximum(w, 1e-30)) * i * xc
    pitch = _segment_pitch(seg)
    for s in range(nseg):
        a_ref[s * pitch:s * pitch + seg, :] = a[s * seg:(s + 1) * seg]
        u_ref[s * pitch:s * pitch + seg, :] = u[s * seg:(s + 1) * seg]

    hs = jnp.zeros((nseg, LRU_BLOCK_DIM), F32)
    ps = jnp.ones((nseg, LRU_BLOCK_DIM), F32)
    for j in range(seg):
        rows = pl.ds(j, nseg, stride=pitch)
        aj = a_ref[rows, :]
        hs = aj * hs + u_ref[rows, :]
        ps = aj * ps
        u_ref[rows, :] = hs
        a_ref[rows, :] = ps
    c = carry_ref[...]
    h = []
    for s in range(nseg):
        h.append(u_ref[s * pitch:s * pitch + seg, :] + a_ref[s * pitch:s * pitch + seg, :] * c)
        c = ps[s:s + 1] * c + hs[s:s + 1]
    carry_ref[...] = c
    h = jnp.concatenate(h, axis=0)
    o_ref[...] = (h * jax.nn.gelu(y_ref[...].astype(F32))).astype(o_ref.dtype)


def _segment_pitch(seg):
    pitch = -(-seg // 4) * 4
    while (pitch // 4) % 2 == 0:
        pitch += 4
    return pitch


def _rglru(z, conv_w, conv_b, wa, ba, wx, bx, lam, batch, seq):
    t = 512
    nb = seq // t
    cwid = LRU_BLOCK_DIM
    scan_rows = SUBLANES * _segment_pitch(t // SUBLANES)
    xb = OFF_LX // cwid
    yb = OFF_LY // cwid
    w_cat = jnp.concatenate([wa, wx], axis=-1).astype(BF16)
    b_cat = jnp.concatenate([ba.reshape(LRU_BLOCKS, 1, cwid),
                             bx.reshape(LRU_BLOCKS, 1, cwid)], axis=-1)
    return pl.pallas_call(
        functools.partial(_lru_kernel, t=t),
        grid=(batch, LRU_BLOCKS, nb),
        in_specs=[
            pl.BlockSpec((t, cwid), lambda b, c, n: (b * nb + n, xb + c)),
            pl.BlockSpec((t, cwid), lambda b, c, n: (b * nb + n, yb + c)),
            pl.BlockSpec((CONV_WIDTH, cwid), lambda b, c, n: (0, c)),
            pl.BlockSpec((1, cwid), lambda b, c, n: (0, c)),
            pl.BlockSpec((1, cwid, 2 * cwid), lambda b, c, n: (c, 0, 0)),
            pl.BlockSpec((1, 1, 2 * cwid), lambda b, c, n: (c, 0, 0)),
            pl.BlockSpec((1, cwid), lambda b, c, n: (0, c)),
        ],
        out_specs=pl.BlockSpec((t, cwid), lambda b, c, n: (b * nb + n, c)),
        out_shape=jax.ShapeDtypeStruct((batch * seq, LRU_WIDTH), BF16),
        scratch_shapes=[pltpu.VMEM((SUBLANES + t, cwid), F32),
                        pltpu.VMEM((scan_rows, cwid), F32), pltpu.VMEM((scan_rows, cwid), F32),
                        pltpu.VMEM((1, cwid), F32)],
        compiler_params=_params(("parallel", "parallel", "arbitrary")),
        name="rglru",
    )(z, z, conv_w, conv_b[None, :], w_cat, b_cat, lam[None, :])


def _merge_kernel(a0_ref, a1_ref, a2_ref, w_ref, g0_ref, g1_ref, g2_ref, o_ref):
    acc = None
    for k, (a_ref, g_ref) in enumerate(((a0_ref, g0_ref), (a1_ref, g1_ref), (a2_ref, g2_ref))):
        d = jnp.dot(a_ref[...], w_ref[k], preferred_element_type=F32)
        term = jax.nn.sigmoid(g_ref[...].astype(F32)) * d
        acc = term if acc is None else acc + term
    o_ref[...] = acc.astype(o_ref.dtype)


def _merge(o_ret, o_swa, o_lru, w_branch, layer, z):
    m = o_ret.shape[0]
    tm, tn = 1024, 512
    gb = OFF_GATE // tn
    gstep = D_MODEL // tn
    a_spec = pl.BlockSpec((tm, BRANCH_WIDTH), lambda i, j: (i, 0))

    def gate_spec(k):
        return pl.BlockSpec((tm, tn), lambda i, j: (i, gb + k * gstep + j))

    return pl.pallas_call(
        _merge_kernel,
        grid=(m // tm, D_MODEL // tn),
        in_specs=[a_spec, a_spec, a_spec,
                  pl.BlockSpec((None, N_BRANCH, BRANCH_WIDTH, tn), lambda i, j: (layer, 0, 0, j)),
                  gate_spec(0), gate_spec(1), gate_spec(2)],
        out_specs=pl.BlockSpec((tm, tn), lambda i, j: (i, j)),
        out_shape=jax.ShapeDtypeStruct((m, D_MODEL), BF16),
        compiler_params=_params(("parallel", "parallel")),
        name="merge",
    )(o_ret, o_swa, o_lru, w_branch, z, z, z)


def kernel(x, positions, norm_mix, w_in, swa_q_gain, swa_k_gain, swa_sinks, conv_w, conv_b,
           lru_wa, lru_ba, lru_wx, lru_bx, lru_lambda, w_branch, w_out, norm_mlp,
           w_mlp_in, w_mlp_out):
    batch, seq, d = x.shape
    m = batch * seq
    xf = x.reshape(m, d)
    rc, rs, sc, ss = _rope_tables(positions.reshape(m, 1))
    w_in, w_branch, w_out, w_mlp_in, w_mlp_out = (
        w.astype(BF16) for w in (w_in, w_branch, w_out, w_mlp_in, w_mlp_out))
    for l in range(DEPTH):
        h = _rmsnorm(xf, norm_mix[l])
        z = _matmul(h, w_in, l, name="in_proj")
        o_ret = _retention(z, rc, rs, batch, seq)
        o_swa = _swa(z, sc, ss, swa_q_gain[l], swa_k_gain[l], swa_sinks[l], batch, seq)
        o_lru = _rglru(z, conv_w[l], conv_b[l], lru_wa[l], lru_ba[l], lru_wx[l], lru_bx[l],
                       lru_lambda[l], batch, seq)
        mixed = _merge(o_ret, o_swa, o_lru, w_branch, l, z)
        xf = _matmul(mixed, w_out, l, epilogue="residual", residual=xf, name="out_proj")
        h = _rmsnorm(xf, norm_mlp[l])
        u = _matmul(h, w_mlp_in, l, epilogue="relu2", name="mlp_in")
        xf = _matmul(u, w_mlp_out, l, epilogue="residual", residual=xf, tk=2048,
                     name="mlp_out")
    return xf.reshape(batch, seq, d)
```

```python
import functools

import jax
import jax.numpy as jnp
from jax import lax
from jax.experimental import pallas as pl
from jax.experimental.pallas import tpu as pltpu

F32 = jnp.float32
BF16 = jnp.bfloat16

D_MODEL = 4096
DEPTH = 2
HEAD_DIM = 128
RET_HEADS = D_MODEL // 512
RET_DK = HEAD_DIM
RET_DV = 2 * HEAD_DIM
RET_THETA = 10000.0
SWA_Q_HEADS = D_MODEL // 256
SWA_KV_HEADS = SWA_Q_HEADS // 4
SWA_GROUP = SWA_Q_HEADS // SWA_KV_HEADS
SWA_WINDOW = 128
ROPE_THETA = 500000.0
ROPE_DIM = HEAD_DIM // 4
LRU_WIDTH = D_MODEL // 2
LRU_BLOCKS = 16
LRU_BLOCK_DIM = LRU_WIDTH // LRU_BLOCKS
CONV_WIDTH = 4
LRU_C = 8.0
N_BRANCH = 3
BRANCH_WIDTH = D_MODEL // 2
D_FF = 4 * D_MODEL
EPS = 1e-6

OFF_RQ = 0
OFF_RK = OFF_RQ + RET_HEADS * RET_DK
OFF_RV = OFF_RK + RET_HEADS * RET_DK
OFF_RG = OFF_RV + RET_HEADS * RET_DV
OFF_SQ = OFF_RG + RET_HEADS * RET_DV
OFF_SK = OFF_SQ + SWA_Q_HEADS * HEAD_DIM
OFF_SV = OFF_SK + SWA_KV_HEADS * HEAD_DIM
OFF_LX = OFF_SV + SWA_KV_HEADS * HEAD_DIM
OFF_LY = OFF_LX + LRU_WIDTH
OFF_GATE = OFF_LY + LRU_WIDTH
IN_TOTAL = OFF_GATE + N_BRANCH * D_MODEL

LANES = 128
SUBLANES = 8
VMEM_LIMIT = 56 * 1024 * 1024


def _params(semantics, vmem=VMEM_LIMIT):
    return pltpu.CompilerParams(dimension_semantics=semantics, vmem_limit_bytes=vmem)


def _sigmoid(x):
    return 0.5 * jnp.tanh(0.5 * x) + 0.5


def _rope_tables_kernel(pos_ref, fr_ref, sr_ref, fs_ref, rc_ref, rs_ref, sc_ref, ss_ref):
    pos = pos_ref[...].astype(F32)
    ang_r = pos * fr_ref[...]
    rc_ref[...] = jnp.cos(ang_r)
    rs_ref[...] = jnp.sin(ang_r) * sr_ref[...]
    ang_s = pos * fs_ref[...]
    sc_ref[...] = jnp.cos(ang_s)
    ss_ref[...] = jnp.sin(ang_s)


def _rope_tables(pos):
    m = pos.shape[0]
    t = 1024
    half_r = RET_DK // 2
    f_r = RET_THETA ** (-jnp.arange(half_r, dtype=F32) / half_r)
    fr = jnp.concatenate([f_r, f_r])[None, :]
    sr = jnp.concatenate([-jnp.ones((half_r,), F32), jnp.ones((half_r,), F32)])[None, :]
    half_s = ROPE_DIM // 2
    f_s = ROPE_THETA ** (-jnp.arange(half_s, dtype=F32) / half_s)
    zeros_rest = jnp.zeros((HEAD_DIM - ROPE_DIM,), F32)
    fs = jnp.concatenate([f_s, f_s, zeros_rest])[None, :]
    row = pl.BlockSpec((1, LANES), lambda i: (0, 0))
    tab = pl.BlockSpec((t, LANES), lambda i: (i, 0))
    return pl.pallas_call(
        _rope_tables_kernel,
        grid=(m // t,),
        in_specs=[pl.BlockSpec((t, 1), lambda i: (i, 0)), row, row, row],
        out_specs=[tab] * 4,
        out_shape=[jax.ShapeDtypeStruct((m, LANES), F32)] * 4,
        compiler_params=_params(("parallel",)),
        name="rope_tables",
    )(pos, fr, sr, fs)


def _rmsnorm_kernel(x_ref, g_ref, o_ref):
    x = x_ref[...]
    ms = jnp.mean(x * x, axis=-1, keepdims=True)
    o_ref[...] = (x * lax.rsqrt(ms + EPS) * g_ref[...]).astype(o_ref.dtype)


def _rmsnorm(x, gain):
    m, d = x.shape
    t = 256
    return pl.pallas_call(
        _rmsnorm_kernel,
        grid=(m // t,),
        in_specs=[pl.BlockSpec((t, d), lambda i: (i, 0)),
                  pl.BlockSpec((1, d), lambda i: (0, 0))],
        out_specs=pl.BlockSpec((t, d), lambda i: (i, 0)),
        out_shape=jax.ShapeDtypeStruct((m, d), BF16),
        compiler_params=_params(("parallel",)),
        name="rmsnorm",
    )(x, gain[None, :])


def _mm_kernel(a_ref, b_ref, *rest, nk, epilogue):
    if epilogue == "residual":
        r_ref, o_ref = rest
        if nk == 1:
            o_ref[...] = r_ref[...] + jnp.dot(a_ref[...], b_ref[...].astype(BF16), preferred_element_type=F32)
        else:
            @pl.when(pl.program_id(2) == 0)
            def _():
                o_ref[...] = r_ref[...]

            o_ref[...] += jnp.dot(a_ref[...], b_ref[...].astype(BF16), preferred_element_type=F32)
    else:
        (o_ref,) = rest
        d = jnp.dot(a_ref[...], b_ref[...].astype(BF16), preferred_element_type=F32)
        if epilogue == "relu2":
            d = jnp.maximum(d, 0.0)
            d = d * d
        o_ref[...] = d.astype(o_ref.dtype)


def _matmul(a, b, layer, *, epilogue="cast", residual=None, tm=1024, tn=1024, tk=None,
            single_buffer_a=False, name="matmul"):
    m, kdim = a.shape
    n = b.shape[2]
    tk = kdim if tk is None else tk
    nk = kdim // tk
    assert m % tm == 0 and n % tn == 0 and kdim % tk == 0
    assert nk == 1 or epilogue == "residual"
    a_mode = dict(pipeline_mode=pl.Buffered(1)) if single_buffer_a else {}
    in_specs = [pl.BlockSpec((tm, tk), lambda i, j, k: (i, k), **a_mode),
                pl.BlockSpec((None, tk, tn), lambda i, j, k: (layer, k, j))]
    args = [a, b]
    if epilogue == "residual":
        in_specs.append(pl.BlockSpec((tm, tn), lambda i, j, k: (i, j)))
        args.append(residual)
        out_dtype = F32
    else:
        out_dtype = BF16
    return pl.pallas_call(
        functools.partial(_mm_kernel, nk=nk, epilogue=epilogue),
        grid=(m // tm, n // tn, nk),
        in_specs=in_specs,
        out_specs=pl.BlockSpec((tm, tn), lambda i, j, k: (i, j)),
        out_shape=jax.ShapeDtypeStruct((m, n), out_dtype),
        compiler_params=_params(("parallel", "parallel", "arbitrary")),
        name=name,
    )(*args)


def _ret_kernel(q_ref, k_ref, v_ref, g_ref, rc_ref, rs_ref, lg_ref, o_ref,
                state_ref, dmat_ref, *, t):
    n = pl.program_id(2)
    lg = lg_ref[0][:, :1]
    ii = lax.broadcasted_iota(jnp.int32, (t, 1), 0).astype(F32)

    @pl.when(n == 0)
    def _():
        state_ref[...] = jnp.zeros_like(state_ref)
        r = lax.broadcasted_iota(jnp.int32, (t, t), 0)
        c = lax.broadcasted_iota(jnp.int32, (t, t), 1)
        diff = (r - c).astype(F32)
        dmat_ref[...] = jnp.where(diff >= 0.0, jnp.exp(lg * jnp.maximum(diff, 0.0)), 0.0)

    rc = rc_ref[...]
    rs = rs_ref[...]
    q = q_ref[...].astype(F32)
    k = k_ref[...].astype(F32)
    qr = q * rc + pltpu.roll(q, RET_DK // 2, 1) * rs
    kr = (k * rc + pltpu.roll(k, RET_DK // 2, 1) * rs) * (RET_DK ** -0.5)
    v = v_ref[...]
    s = lax.dot_general(qr.astype(BF16), kr.astype(BF16), (((1,), (1,)), ((), ())),
                        preferred_element_type=F32)
    p = (s * dmat_ref[...]).astype(BF16)
    intra = jnp.dot(p, v, preferred_element_type=F32)
    xi = jnp.exp(lg * (ii + 1.0))
    st = state_ref[...]
    cross = jnp.dot((qr * xi).astype(BF16), st.astype(BF16), preferred_element_type=F32)
    zeta = jnp.exp(lg * (t - 1.0 - ii))
    kz = (kr * zeta).astype(BF16)
    kv = lax.dot_general(kz, v, (((0,), (0,)), ((), ())), preferred_element_type=F32)
    state_ref[...] = jnp.exp(lg * float(t)) * st + kv
    y = intra + cross
    y = y * lax.rsqrt(jnp.mean(y * y, axis=-1, keepdims=True) + EPS)
    g = g_ref[...].astype(F32)
    o_ref[...] = (g * _sigmoid(g) * y).astype(o_ref.dtype)


def _retention(z, rc, rs, batch, seq):
    t = 512
    nb = seq // t
    log_g = jnp.log1p(-jnp.exp2(-5.0 - jnp.arange(RET_HEADS, dtype=F32)))
    lg = jnp.broadcast_to(log_g[:, None, None], (RET_HEADS, 1, LANES))
    kb = OFF_RK // RET_DK
    vb = OFF_RV // RET_DV
    gb = OFF_RG // RET_DV
    return pl.pallas_call(
        functools.partial(_ret_kernel, t=t),
        grid=(batch, RET_HEADS, nb),
        in_specs=[
            pl.BlockSpec((t, RET_DK), lambda b, h, n: (b * nb + n, h)),
            pl.BlockSpec((t, RET_DK), lambda b, h, n: (b * nb + n, kb + h)),
            pl.BlockSpec((t, RET_DV), lambda b, h, n: (b * nb + n, vb + h)),
            pl.BlockSpec((t, RET_DV), lambda b, h, n: (b * nb + n, gb + h)),
            pl.BlockSpec((t, LANES), lambda b, h, n: (b * nb + n, 0)),
            pl.BlockSpec((t, LANES), lambda b, h, n: (b * nb + n, 0)),
            pl.BlockSpec((1, 1, LANES), lambda b, h, n: (h, 0, 0)),
        ],
        out_specs=pl.BlockSpec((t, RET_DV), lambda b, h, n: (b * nb + n, h)),
        out_shape=jax.ShapeDtypeStruct((batch * seq, RET_HEADS * RET_DV), BF16),
        scratch_shapes=[pltpu.VMEM((RET_DK, RET_DV), F32), pltpu.VMEM((t, t), F32)],
        compiler_params=_params(("parallel", "parallel", "arbitrary")),
        name="retention",
    )(z, z, z, z, rc, rs, lg)


def _norm_rope(x, gain, c, s, ones, rot):
    xf = x.astype(F32)
    ss = jnp.dot((xf * xf).astype(BF16), ones, preferred_element_type=F32)
    xn = xf * lax.rsqrt(ss * (1.0 / HEAD_DIM) + EPS) * gain
    partner = jnp.dot(xn.astype(BF16), rot, preferred_element_type=F32)
    return xn * c + partner * s


def _swa_kernel(q_ref, kc_ref, kp_ref, vc_ref, vp_ref, c_ref, s_ref, cp_ref, sp_ref,
                qg_ref, kg_ref, sink_ref, rot_ref, o_ref, *, tq):
    n = pl.program_id(2)
    w = SWA_WINDOW
    group_rows = SWA_GROUP * w
    ones = jnp.ones((HEAD_DIM, HEAD_DIM), BF16)
    rot = rot_ref[...]
    qg = qg_ref[...] * (HEAD_DIM ** -0.5)
    kg = kg_ref[...]
    c = c_ref[...]
    s = s_ref[...]
    k_prev = _norm_rope(kp_ref[...], kg, cp_ref[...], sp_ref[...], ones, rot)
    k_cur = _norm_rope(kc_ref[...], kg, c, s, ones, rot)
    k_all = jnp.concatenate([k_prev, k_cur], axis=0).astype(BF16)
    v_all = jnp.concatenate([vp_ref[...], vc_ref[...]], axis=0)
    v_ext = jnp.concatenate([v_all, jnp.ones_like(v_all)], axis=1)
    sink = sink_ref[0]
    qi = lax.broadcasted_iota(jnp.int32, (w, 2 * w), 0)
    sj = lax.broadcasted_iota(jnp.int32, (w, 2 * w), 1)
    band = (sj > qi) & (sj <= qi + w)
    bias = jnp.where(band, 0.0, -jnp.inf)
    bias_first = jnp.where(band & (sj >= jnp.where(n > 0, 0, w)), 0.0, -jnp.inf)
    nt = tq // w
    xq = jnp.concatenate([q_ref[t * w:(t + 1) * w, g * HEAD_DIM:(g + 1) * HEAD_DIM]
                          for t in range(nt) for g in range(SWA_GROUP)], axis=0)
    c4 = jnp.concatenate([c[t * w:(t + 1) * w] for t in range(nt)
                          for _ in range(SWA_GROUP)], axis=0)
    s4 = jnp.concatenate([s[t * w:(t + 1) * w] for t in range(nt)
                          for _ in range(SWA_GROUP)], axis=0)
    qs = _norm_rope(xq, qg, c4, s4, ones, rot).astype(BF16)
    sc = jnp.concatenate(
        [(lax.dot_general(qs[t * group_rows:(t + 1) * group_rows], k_all[t * w:(t + 2) * w],
                          (((1,), (1,)), ((), ())), preferred_element_type=F32
                          ).reshape(SWA_GROUP, w, 2 * w)
          + (bias_first if t == 0 else bias)[None]).reshape(group_rows, 2 * w)
         for t in range(nt)], axis=0)
    sink_all = jnp.concatenate([sink] * nt, axis=0)
    m = jnp.maximum(jnp.max(sc, axis=-1, keepdims=True), sink_all[:, :1])
    p = jnp.exp(sc - m).astype(BF16)
    o_ext = jnp.concatenate(
        [jnp.dot(p[t * group_rows:(t + 1) * group_rows], v_ext[t * w:(t + 2) * w],
                 preferred_element_type=F32) for t in range(nt)], axis=0)
    denom = o_ext[:, HEAD_DIM:] + jnp.exp(sink_all - m)
    o = (o_ext[:, :HEAD_DIM] / denom).astype(o_ref.dtype)
    for t in range(nt):
        for g in range(SWA_GROUP):
            r0 = (t * SWA_GROUP + g) * w
            o_ref[t * w:(t + 1) * w, g * HEAD_DIM:(g + 1) * HEAD_DIM] = o[r0:r0 + w]


def _rotate_half_matrix():
    half = ROPE_DIM // 2
    src = jnp.arange(HEAD_DIM)[:, None]
    dst = jnp.arange(HEAD_DIM)[None, :]
    minus = (dst < half) & (src == dst + half)
    plus = (dst >= half) & (dst < ROPE_DIM) & (src == dst - half)
    return (plus.astype(F32) - minus.astype(F32)).astype(BF16)


def _swa(z, sc, ss, q_gain, k_gain, sinks, batch, seq):
    tq = 512
    w = SWA_WINDOW
    nb = seq // tq
    r = tq // w
    gw = SWA_GROUP * HEAD_DIM
    qb = OFF_SQ // gw
    kb = OFF_SK // HEAD_DIM
    vb = OFF_SV // HEAD_DIM
    sink_b = jnp.broadcast_to(
        jnp.repeat(sinks.astype(F32).reshape(SWA_KV_HEADS, SWA_GROUP), w, axis=1)[:, :, None],
        (SWA_KV_HEADS, SWA_GROUP * w, LANES))

    def cur(b, h, n):
        return b * nb + n

    def prev(b, h, n):
        return b * nb * r + jnp.maximum(n * r - 1, 0)

    return pl.pallas_call(
        functools.partial(_swa_kernel, tq=tq),
        grid=(batch, SWA_KV_HEADS, nb),
        in_specs=[
            pl.BlockSpec((tq, gw), lambda b, h, n: (cur(b, h, n), qb + h)),
            pl.BlockSpec((tq, HEAD_DIM), lambda b, h, n: (cur(b, h, n), kb + h)),
            pl.BlockSpec((w, HEAD_DIM), lambda b, h, n: (prev(b, h, n), kb + h)),
            pl.BlockSpec((tq, HEAD_DIM), lambda b, h, n: (cur(b, h, n), vb + h)),
            pl.BlockSpec((w, HEAD_DIM), lambda b, h, n: (prev(b, h, n), vb + h)),
            pl.BlockSpec((tq, LANES), lambda b, h, n: (cur(b, h, n), 0)),
            pl.BlockSpec((tq, LANES), lambda b, h, n: (cur(b, h, n), 0)),
            pl.BlockSpec((w, LANES), lambda b, h, n: (prev(b, h, n), 0)),
            pl.BlockSpec((w, LANES), lambda b, h, n: (prev(b, h, n), 0)),
            pl.BlockSpec((1, HEAD_DIM), lambda b, h, n: (0, 0)),
            pl.BlockSpec((1, HEAD_DIM), lambda b, h, n: (0, 0)),
            pl.BlockSpec((1, SWA_GROUP * w, LANES), lambda b, h, n: (h, 0, 0)),
            pl.BlockSpec((HEAD_DIM, HEAD_DIM), lambda b, h, n: (0, 0)),
        ],
        out_specs=pl.BlockSpec((tq, gw), lambda b, h, n: (cur(b, h, n), h)),
        out_shape=jax.ShapeDtypeStruct((batch * seq, SWA_Q_HEADS * HEAD_DIM), BF16),
        compiler_params=_params(("parallel", "parallel", "parallel")),
        name="swa",
    )(z, z, z, z, z, sc, ss, sc, ss, q_gain[None, :], k_gain[None, :], sink_b,
      _rotate_half_matrix())


def _lru_kernel(x_ref, y_ref, cw_ref, cb_ref, w_ref, b_ref, lam_ref, o_ref,
                xs_ref, a_ref, u_ref, carry_ref, *, t):
    n = pl.program_id(2)
    pad = SUBLANES
    nseg = SUBLANES
    seg = t // nseg

    @pl.when(n == 0)
    def _():
        xs_ref[0:pad, :] = jnp.zeros((pad, LRU_BLOCK_DIM), F32)
        carry_ref[...] = jnp.zeros_like(carry_ref)

    xs_ref[pad:pad + t, :] = x_ref[...].astype(F32)
    cw = cw_ref[...]
    xc = cb_ref[...] + xs_ref[pad:pad + t, :] * cw[CONV_WIDTH - 1:CONV_WIDTH]
    for d in range(1, CONV_WIDTH):
        xc = xc + xs_ref[pl.ds(pad - d, t), :] * cw[CONV_WIDTH - 1 - d:CONV_WIDTH - d]
    xs_ref[0:pad, :] = xs_ref[t:t + pad, :]

    gates = jnp.dot(xc.astype(BF16), w_ref[0], preferred_element_type=F32) + b_ref[0]
    r = _sigmoid(gates[:, :LRU_BLOCK_DIM])
    i = _sigmoid(gates[:, LRU_BLOCK_DIM:])
    log_a = -LRU_C * r * jax.nn.softplus(-lam_ref[...])
    a = jnp.exp(log_a)
    w = -jnp.tanh(log_a) * (1.0 + a * a)
    u = w * lax.rsqrt(jnp.maximum(w, 1e-30)) * i * xc
    pitch = _segment_pitch(seg)
    for s in range(nseg):
        a_ref[s * pitch:s * pitch + seg, :] = a[s * seg:(s + 1) * seg]
        u_ref[s * pitch:s * pitch + seg, :] = u[s * seg:(s + 1) * seg]

    hs = jnp.zeros((nseg, LRU_BLOCK_DIM), F32)
    ps = jnp.ones((nseg, LRU_BLOCK_DIM), F32)
    for j in range(seg):
        rows = pl.ds(j, nseg, stride=pitch)
        aj = a_ref[rows, :]
        hs = aj * hs + u_ref[rows, :]
        ps = aj * ps
        u_ref[rows, :] = hs
        a_ref[rows, :] = ps
    c = carry_ref[...]
    h = []
    for s in range(nseg):
        h.append(u_ref[s * pitch:s * pitch + seg, :] + a_ref[s * pitch:s * pitch + seg, :] * c)
        c = ps[s:s + 1] * c + hs[s:s + 1]
    carry_ref[...] = c
    h = jnp.concatenate(h, axis=0)
    o_ref[...] = (h * jax.nn.gelu(y_ref[...].astype(F32))).astype(o_ref.dtype)


def _segment_pitch(seg):
    pitch = -(-seg // 4) * 4
    while (pitch // 4) % 2 == 0:
        pitch += 4
    return pitch


def _rglru(z, conv_w, conv_b, wa, ba, wx, bx, lam, batch, seq):
    t = 2048
    nb = seq // t
    cwid = LRU_BLOCK_DIM
    scan_rows = SUBLANES * _segment_pitch(t // SUBLANES)
    xb = OFF_LX // cwid
    yb = OFF_LY // cwid
    w_cat = jnp.concatenate([wa, wx], axis=-1).astype(BF16)
    b_cat = jnp.concatenate([ba.reshape(LRU_BLOCKS, 1, cwid),
                             bx.reshape(LRU_BLOCKS, 1, cwid)], axis=-1)
    return pl.pallas_call(
        functools.partial(_lru_kernel, t=t),
        grid=(batch, LRU_BLOCKS, nb),
        in_specs=[
            pl.BlockSpec((t, cwid), lambda b, c, n: (b * nb + n, xb + c)),
            pl.BlockSpec((t, cwid), lambda b, c, n: (b * nb + n, yb + c)),
            pl.BlockSpec((CONV_WIDTH, cwid), lambda b, c, n: (0, c)),
            pl.BlockSpec((1, cwid), lambda b, c, n: (0, c)),
            pl.BlockSpec((1, cwid, 2 * cwid), lambda b, c, n: (c, 0, 0)),
            pl.BlockSpec((1, 1, 2 * cwid), lambda b, c, n: (c, 0, 0)),
            pl.BlockSpec((1, cwid), lambda b, c, n: (0, c)),
        ],
        out_specs=pl.BlockSpec((t, cwid), lambda b, c, n: (b * nb + n, c)),
        out_shape=jax.ShapeDtypeStruct((batch * seq, LRU_WIDTH), BF16),
        scratch_shapes=[pltpu.VMEM((SUBLANES + t, cwid), F32),
                        pltpu.VMEM((scan_rows, cwid), F32), pltpu.VMEM((scan_rows, cwid), F32),
                        pltpu.VMEM((1, cwid), F32)],
        compiler_params=_params(("parallel", "parallel", "arbitrary")),
        name="rglru",
    )(z, z, conv_w, conv_b[None, :], w_cat, b_cat, lam[None, :])


def _merge_kernel(a0_ref, a1_ref, a2_ref, w_ref, g0_ref, g1_ref, g2_ref, o_ref):
    acc = None
    for k, (a_ref, g_ref) in enumerate(((a0_ref, g0_ref), (a1_ref, g1_ref), (a2_ref, g2_ref))):
        d = jnp.dot(a_ref[...], w_ref[k], preferred_element_type=F32)
        term = _sigmoid(g_ref[...].astype(F32)) * d
        acc = term if acc is None else acc + term
    o_ref[...] = acc.astype(o_ref.dtype)


def _merge(o_ret, o_swa, o_lru, w_branch, layer, z):
    m = o_ret.shape[0]
    tm, tn = 1024, 512
    gb = OFF_GATE // tn
    gstep = D_MODEL // tn
    a_spec = pl.BlockSpec((tm, BRANCH_WIDTH), lambda i, j: (i, 0))

    def gate_spec(k):
        return pl.BlockSpec((tm, tn), lambda i, j: (i, gb + k * gstep + j))

    return pl.pallas_call(
        _merge_kernel,
        grid=(m // tm, D_MODEL // tn),
        in_specs=[a_spec, a_spec, a_spec,
                  pl.BlockSpec((None, N_BRANCH, BRANCH_WIDTH, tn), lambda i, j: (layer, 0, 0, j)),
                  gate_spec(0), gate_spec(1), gate_spec(2)],
        out_specs=pl.BlockSpec((tm, tn), lambda i, j: (i, j)),
        out_shape=jax.ShapeDtypeStruct((m, D_MODEL), BF16),
        compiler_params=_params(("parallel", "parallel")),
        name="merge",
    )(o_ret, o_swa, o_lru, w_branch, z, z, z)


def kernel(x, positions, norm_mix, w_in, swa_q_gain, swa_k_gain, swa_sinks, conv_w, conv_b,
           lru_wa, lru_ba, lru_wx, lru_bx, lru_lambda, w_branch, w_out, norm_mlp,
           w_mlp_in, w_mlp_out):
    batch, seq, d = x.shape
    m = batch * seq
    xf = x.reshape(m, d)
    rc, rs, sc, ss = _rope_tables(positions.reshape(m, 1))
    w_branch, w_out, w_mlp_out = (w.astype(BF16) for w in (w_branch, w_out, w_mlp_out))
    for l in range(DEPTH):
        h = _rmsnorm(xf, norm_mix[l])
        z = _matmul(h, w_in, l, tm=2048, tn=512, single_buffer_a=True, name="in_proj")
        o_ret = _retention(z, rc, rs, batch, seq)
        o_swa = _swa(z, sc, ss, swa_q_gain[l], swa_k_gain[l], swa_sinks[l], batch, seq)
        o_lru = _rglru(z, conv_w[l], conv_b[l], lru_wa[l], lru_ba[l], lru_wx[l], lru_bx[l],
                       lru_lambda[l], batch, seq)
        mixed = _merge(o_ret, o_swa, o_lru, w_branch, l, z)
        xf = _matmul(mixed, w_out, l, epilogue="residual", residual=xf, name="out_proj")
        h = _rmsnorm(xf, norm_mlp[l])
        u = _matmul(h, w_mlp_in, l, epilogue="relu2", tm=2048, tn=512, single_buffer_a=True,
                    name="mlp_in")
        xf = _matmul(u, w_mlp_out, l, epilogue="residual", residual=xf, tk=2048,
                     name="mlp_out")
    return xf.reshape(batch, seq, d)
```

```python
import functools

import jax
import jax.numpy as jnp
from jax import lax
from jax.experimental import pallas as pl
from jax.experimental.pallas import tpu as pltpu

F32 = jnp.float32
BF16 = jnp.bfloat16

D_MODEL = 4096
DEPTH = 2
HEAD_DIM = 128
RET_HEADS = D_MODEL // 512
RET_DK = HEAD_DIM
RET_DV = 2 * HEAD_DIM
RET_THETA = 10000.0
SWA_Q_HEADS = D_MODEL // 256
SWA_KV_HEADS = SWA_Q_HEADS // 4
SWA_GROUP = SWA_Q_HEADS // SWA_KV_HEADS
SWA_WINDOW = 128
ROPE_THETA = 500000.0
ROPE_DIM = HEAD_DIM // 4
LRU_WIDTH = D_MODEL // 2
LRU_BLOCKS = 16
LRU_BLOCK_DIM = LRU_WIDTH // LRU_BLOCKS
CONV_WIDTH = 4
LRU_C = 8.0
N_BRANCH = 3
BRANCH_WIDTH = D_MODEL // 2
D_FF = 4 * D_MODEL
EPS = 1e-6

OFF_RQ = 0
OFF_RK = OFF_RQ + RET_HEADS * RET_DK
OFF_RV = OFF_RK + RET_HEADS * RET_DK
OFF_RG = OFF_RV + RET_HEADS * RET_DV
OFF_SQ = OFF_RG + RET_HEADS * RET_DV
OFF_SK = OFF_SQ + SWA_Q_HEADS * HEAD_DIM
OFF_SV = OFF_SK + SWA_KV_HEADS * HEAD_DIM
OFF_LX = OFF_SV + SWA_KV_HEADS * HEAD_DIM
OFF_LY = OFF_LX + LRU_WIDTH
OFF_GATE = OFF_LY + LRU_WIDTH
IN_TOTAL = OFF_GATE + N_BRANCH * D_MODEL

LANES = 128
SUBLANES = 8
VMEM_LIMIT = 56 * 1024 * 1024


def _params(semantics, vmem=VMEM_LIMIT):
    return pltpu.CompilerParams(dimension_semantics=semantics, vmem_limit_bytes=vmem)


def _sigmoid(x):
    return 0.5 * jnp.tanh(0.5 * x) + 0.5


def _rope_tables_kernel(pos_ref, fr_ref, sr_ref, fs_ref, rc_ref, rs_ref, sc_ref, ss_ref):
    pos = pos_ref[...].astype(F32)
    ang_r = pos * fr_ref[...]
    rc_ref[...] = jnp.cos(ang_r)
    rs_ref[...] = jnp.sin(ang_r) * sr_ref[...]
    ang_s = pos * fs_ref[...]
    sc_ref[...] = jnp.cos(ang_s)
    ss_ref[...] = jnp.sin(ang_s)


def _rope_tables(pos):
    m = pos.shape[0]
    t = 1024
    half_r = RET_DK // 2
    f_r = RET_THETA ** (-jnp.arange(half_r, dtype=F32) / half_r)
    fr = jnp.concatenate([f_r, f_r])[None, :]
    sr = jnp.concatenate([-jnp.ones((half_r,), F32), jnp.ones((half_r,), F32)])[None, :]
    half_s = ROPE_DIM // 2
    f_s = ROPE_THETA ** (-jnp.arange(half_s, dtype=F32) / half_s)
    zeros_rest = jnp.zeros((HEAD_DIM - ROPE_DIM,), F32)
    fs = jnp.concatenate([f_s, f_s, zeros_rest])[None, :]
    row = pl.BlockSpec((1, LANES), lambda i: (0, 0))
    tab = pl.BlockSpec((t, LANES), lambda i: (i, 0))
    return pl.pallas_call(
        _rope_tables_kernel,
        grid=(m // t,),
        in_specs=[pl.BlockSpec((t, 1), lambda i: (i, 0)), row, row, row],
        out_specs=[tab] * 4,
        out_shape=[jax.ShapeDtypeStruct((m, LANES), F32)] * 4,
        compiler_params=_params(("parallel",)),
        name="rope_tables",
    )(pos, fr, sr, fs)


def _rmsnorm_kernel(x_ref, g_ref, o_ref):
    x = x_ref[...]
    ms = jnp.mean(x * x, axis=-1, keepdims=True)
    o_ref[...] = (x * lax.rsqrt(ms + EPS) * g_ref[...]).astype(o_ref.dtype)


def _rmsnorm(x, gain):
    m, d = x.shape
    t = 256
    return pl.pallas_call(
        _rmsnorm_kernel,
        grid=(m // t,),
        in_specs=[pl.BlockSpec((t, d), lambda i: (i, 0)),
                  pl.BlockSpec((1, d), lambda i: (0, 0))],
        out_specs=pl.BlockSpec((t, d), lambda i: (i, 0)),
        out_shape=jax.ShapeDtypeStruct((m, d), BF16),
        compiler_params=_params(("parallel",)),
        name="rmsnorm",
    )(x, gain[None, :])


def _mm_kernel(a_ref, b_ref, *rest, nk, epilogue):
    if epilogue == "residual":
        r_ref, o_ref = rest
        if nk == 1:
            o_ref[...] = r_ref[...] + jnp.dot(a_ref[...], b_ref[...].astype(BF16), preferred_element_type=F32)
        else:
            @pl.when(pl.program_id(2) == 0)
            def _():
                o_ref[...] = r_ref[...]

            o_ref[...] += jnp.dot(a_ref[...], b_ref[...].astype(BF16), preferred_element_type=F32)
    else:
        (o_ref,) = rest
        d = jnp.dot(a_ref[...], b_ref[...].astype(BF16), preferred_element_type=F32)
        if epilogue == "relu2":
            d = jnp.maximum(d, 0.0)
            d = d * d
        o_ref[...] = d.astype(o_ref.dtype)


def _matmul(a, b, layer, *, epilogue="cast", residual=None, tm=1024, tn=1024, tk=None,
            single_buffer_a=False, name="matmul"):
    m, kdim = a.shape
    n = b.shape[2]
    tk = kdim if tk is None else tk
    nk = kdim // tk
    assert m % tm == 0 and n % tn == 0 and kdim % tk == 0
    assert nk == 1 or epilogue == "residual"
    a_mode = dict(pipeline_mode=pl.Buffered(1)) if single_buffer_a else {}
    in_specs = [pl.BlockSpec((tm, tk), lambda i, j, k: (i, k), **a_mode),
                pl.BlockSpec((None, tk, tn), lambda i, j, k: (layer, k, j))]
    args = [a, b]
    if epilogue == "residual":
        in_specs.append(pl.BlockSpec((tm, tn), lambda i, j, k: (i, j)))
        args.append(residual)
        out_dtype = F32
    else:
        out_dtype = BF16
    return pl.pallas_call(
        functools.partial(_mm_kernel, nk=nk, epilogue=epilogue),
        grid=(m // tm, n // tn, nk),
        in_specs=in_specs,
        out_specs=pl.BlockSpec((tm, tn), lambda i, j, k: (i, j)),
        out_shape=jax.ShapeDtypeStruct((m, n), out_dtype),
        compiler_params=_params(("parallel", "parallel", "arbitrary")),
        name=name,
    )(*args)


def _ret_kernel(q_ref, k_ref, v_ref, g_ref, rc_ref, rs_ref, lg_ref, o_ref,
                state_ref, dmat_ref, *, t):
    n = pl.program_id(2)
    lg = lg_ref[0][:, :1]
    ii = lax.broadcasted_iota(jnp.int32, (t, 1), 0).astype(F32)

    @pl.when(n == 0)
    def _():
        state_ref[...] = jnp.zeros_like(state_ref)
        r = lax.broadcasted_iota(jnp.int32, (t, t), 0)
        c = lax.broadcasted_iota(jnp.int32, (t, t), 1)
        diff = (r - c).astype(F32)
        dmat_ref[...] = jnp.where(diff >= 0.0, jnp.exp(lg * jnp.maximum(diff, 0.0)), 0.0)

    rc = rc_ref[...]
    rs = rs_ref[...]
    q = q_ref[...].astype(F32)
    k = k_ref[...].astype(F32)
    qr = q * rc + pltpu.roll(q, RET_DK // 2, 1) * rs
    kr = (k * rc + pltpu.roll(k, RET_DK // 2, 1) * rs) * (RET_DK ** -0.5)
    v = v_ref[...]
    s = lax.dot_general(qr.astype(BF16), kr.astype(BF16), (((1,), (1,)), ((), ())),
                        preferred_element_type=F32)
    p = (s * dmat_ref[...]).astype(BF16)
    intra = jnp.dot(p, v, preferred_element_type=F32)
    xi = jnp.exp(lg * (ii + 1.0))
    st = state_ref[...]
    cross = jnp.dot((qr * xi).astype(BF16), st.astype(BF16), preferred_element_type=F32)
    zeta = jnp.exp(lg * (t - 1.0 - ii))
    kz = (kr * zeta).astype(BF16)
    kv = lax.dot_general(kz, v, (((0,), (0,)), ((), ())), preferred_element_type=F32)
    state_ref[...] = jnp.exp(lg * float(t)) * st + kv
    y = intra + cross
    y = y * lax.rsqrt(jnp.mean(y * y, axis=-1, keepdims=True) + EPS)
    g = g_ref[...].astype(F32)
    o_ref[...] = (g * _sigmoid(g) * y).astype(o_ref.dtype)


def _retention(z, rc, rs, batch, seq):
    t = 512
    nb = seq // t
    log_g = jnp.log1p(-jnp.exp2(-5.0 - jnp.arange(RET_HEADS, dtype=F32)))
    lg = jnp.broadcast_to(log_g[:, None, None], (RET_HEADS, 1, LANES))
    kb = OFF_RK // RET_DK
    vb = OFF_RV // RET_DV
    gb = OFF_RG // RET_DV
    return pl.pallas_call(
        functools.partial(_ret_kernel, t=t),
        grid=(batch, RET_HEADS, nb),
        in_specs=[
            pl.BlockSpec((t, RET_DK), lambda b, h, n: (b * nb + n, h)),
            pl.BlockSpec((t, RET_DK), lambda b, h, n: (b * nb + n, kb + h)),
            pl.BlockSpec((t, RET_DV), lambda b, h, n: (b * nb + n, vb + h)),
            pl.BlockSpec((t, RET_DV), lambda b, h, n: (b * nb + n, gb + h)),
            pl.BlockSpec((t, LANES), lambda b, h, n: (b * nb + n, 0)),
            pl.BlockSpec((t, LANES), lambda b, h, n: (b * nb + n, 0)),
            pl.BlockSpec((1, 1, LANES), lambda b, h, n: (h, 0, 0)),
        ],
        out_specs=pl.BlockSpec((t, RET_DV), lambda b, h, n: (b * nb + n, h)),
        out_shape=jax.ShapeDtypeStruct((batch * seq, RET_HEADS * RET_DV), BF16),
        scratch_shapes=[pltpu.VMEM((RET_DK, RET_DV), F32), pltpu.VMEM((t, t), F32)],
        compiler_params=_params(("parallel", "parallel", "arbitrary")),
        name="retention",
    )(z, z, z, z, rc, rs, lg)


def _norm_rope(x, gain, c, s, ones, rot):
    xf = x.astype(F32)
    ss = jnp.dot((xf * xf).astype(BF16), ones, preferred_element_type=F32)
    xn = xf * lax.rsqrt(ss * (1.0 / HEAD_DIM) + EPS) * gain
    partner = jnp.dot(xn.astype(BF16), rot, preferred_element_type=F32)
    return xn * c + partner * s


def _swa_kernel(q_ref, kc_ref, kp_ref, vc_ref, vp_ref, c_ref, s_ref, cp_ref, sp_ref,
                qg_ref, kg_ref, sink_ref, rot_ref, o_ref, *, tq):
    n = pl.program_id(2)
    w = SWA_WINDOW
    group_rows = SWA_GROUP * w
    ones = jnp.ones((HEAD_DIM, HEAD_DIM), BF16)
    rot = rot_ref[...]
    qg = qg_ref[...] * (HEAD_DIM ** -0.5)
    kg = kg_ref[...]
    c = c_ref[...]
    s = s_ref[...]
    k_prev = _norm_rope(kp_ref[...], kg, cp_ref[...], sp_ref[...], ones, rot)
    k_cur = _norm_rope(kc_ref[...], kg, c, s, ones, rot)
    k_all = jnp.concatenate([k_prev, k_cur], axis=0).astype(BF16)
    v_all = jnp.concatenate([vp_ref[...], vc_ref[...]], axis=0)
    v_t = v_all.astype(F32).T.astype(BF16)
    ones_t = jnp.ones((HEAD_DIM, 2 * w), BF16)
    sink = sink_ref[0]
    kj = lax.broadcasted_iota(jnp.int32, (2 * w, w), 0)
    qi = lax.broadcasted_iota(jnp.int32, (2 * w, w), 1)
    band = (kj > qi) & (kj <= qi + w)
    bias = jnp.concatenate([jnp.where(band, 0.0, -jnp.inf)] * SWA_GROUP, axis=1)
    bias_first = jnp.concatenate(
        [jnp.where(band & (kj >= jnp.where(n > 0, 0, w)), 0.0, -jnp.inf)] * SWA_GROUP, axis=1)
    nt = tq // w
    xq = jnp.concatenate([q_ref[t * w:(t + 1) * w, g * HEAD_DIM:(g + 1) * HEAD_DIM]
                          for t in range(nt) for g in range(SWA_GROUP)], axis=0)
    c4 = jnp.concatenate([c[t * w:(t + 1) * w] for t in range(nt)
                          for _ in range(SWA_GROUP)], axis=0)
    s4 = jnp.concatenate([s[t * w:(t + 1) * w] for t in range(nt)
                          for _ in range(SWA_GROUP)], axis=0)
    qs = _norm_rope(xq, qg, c4, s4, ones, rot).astype(BF16)
    sc = [lax.dot_general(k_all[t * w:(t + 2) * w], qs[t * group_rows:(t + 1) * group_rows],
                          (((1,), (1,)), ((), ())), preferred_element_type=F32)
          + (bias_first if t == 0 else bias) for t in range(nt)]
    m = [jnp.maximum(jnp.max(x, axis=0, keepdims=True), sink) for x in sc]
    p = [jnp.exp(x - mx).astype(BF16) for x, mx in zip(sc, m)]
    o_ext = [jnp.dot(jnp.concatenate([v_t[:, t * w:(t + 2) * w], ones_t], axis=0), p[t],
                     preferred_element_type=F32) for t in range(nt)]
    for t in range(nt):
        denom = o_ext[t][HEAD_DIM:HEAD_DIM + 1] + jnp.exp(sink - m[t])
        o_t = o_ext[t][:HEAD_DIM] / denom
        for g in range(SWA_GROUP):
            o_ref[t * w:(t + 1) * w, g * HEAD_DIM:(g + 1) * HEAD_DIM] = (
                o_t[:, g * w:(g + 1) * w].T.astype(o_ref.dtype))


def _rotate_half_matrix():
    half = ROPE_DIM // 2
    src = jnp.arange(HEAD_DIM)[:, None]
    dst = jnp.arange(HEAD_DIM)[None, :]
    minus = (dst < half) & (src == dst + half)
    plus = (dst >= half) & (dst < ROPE_DIM) & (src == dst - half)
    return (plus.astype(F32) - minus.astype(F32)).astype(BF16)


def _swa(z, sc, ss, q_gain, k_gain, sinks, batch, seq):
    tq = 512
    w = SWA_WINDOW
    nb = seq // tq
    r = tq // w
    gw = SWA_GROUP * HEAD_DIM
    qb = OFF_SQ // gw
    kb = OFF_SK // HEAD_DIM
    vb = OFF_SV // HEAD_DIM
    sink_b = jnp.repeat(sinks.astype(F32).reshape(SWA_KV_HEADS, SWA_GROUP), w,
                        axis=1)[:, None, :]

    def cur(b, h, n):
        return b * nb + n

    def prev(b, h, n):
        return b * nb * r + jnp.maximum(n * r - 1, 0)

    return pl.pallas_call(
        functools.partial(_swa_kernel, tq=tq),
        grid=(batch, SWA_KV_HEADS, nb),
        in_specs=[
            pl.BlockSpec((tq, gw), lambda b, h, n: (cur(b, h, n), qb + h)),
            pl.BlockSpec((tq, HEAD_DIM), lambda b, h, n: (cur(b, h, n), kb + h)),
            pl.BlockSpec((w, HEAD_DIM), lambda b, h, n: (prev(b, h, n), kb + h)),
            pl.BlockSpec((tq, HEAD_DIM), lambda b, h, n: (cur(b, h, n), vb + h)),
            pl.BlockSpec((w, HEAD_DIM), lambda b, h, n: (prev(b, h, n), vb + h)),
            pl.BlockSpec((tq, LANES), lambda b, h, n: (cur(b, h, n), 0)),
            pl.BlockSpec((tq, LANES), lambda b, h, n: (cur(b, h, n), 0)),
            pl.BlockSpec((w, LANES), lambda b, h, n: (prev(b, h, n), 0)),
            pl.BlockSpec((w, LANES), lambda b, h, n: (prev(b, h, n), 0)),
            pl.BlockSpec((1, HEAD_DIM), lambda b, h, n: (0, 0)),
            pl.BlockSpec((1, HEAD_DIM), lambda b, h, n: (0, 0)),
            pl.BlockSpec((1, 1, SWA_GROUP * w), lambda b, h, n: (h, 0, 0)),
            pl.BlockSpec((HEAD_DIM, HEAD_DIM), lambda b, h, n: (0, 0)),
        ],
        out_specs=pl.BlockSpec((tq, gw), lambda b, h, n: (cur(b, h, n), h)),
        out_shape=jax.ShapeDtypeStruct((batch * seq, SWA_Q_HEADS * HEAD_DIM), BF16),
        compiler_params=_params(("parallel", "parallel", "parallel")),
        name="swa",
    )(z, z, z, z, z, sc, ss, sc, ss, q_gain[None, :], k_gain[None, :], sink_b,
      _rotate_half_matrix())


def _lru_kernel(x_ref, y_ref, cw_ref, cb_ref, w_ref, b_ref, lam_ref, o_ref,
                xs_ref, a_ref, u_ref, carry_ref, *, t):
    n = pl.program_id(2)
    pad = SUBLANES
    nseg = SUBLANES
    seg = t // nseg

    @pl.when(n == 0)
    def _():
        xs_ref[0:pad, :] = jnp.zeros((pad, LRU_BLOCK_DIM), F32)
        carry_ref[...] = jnp.zeros_like(carry_ref)

    xs_ref[pad:pad + t, :] = x_ref[...].astype(F32)
    cw = cw_ref[...]
    xc = cb_ref[...] + xs_ref[pad:pad + t, :] * cw[CONV_WIDTH - 1:CONV_WIDTH]
    for d in range(1, CONV_WIDTH):
        xc = xc + xs_ref[pl.ds(pad - d, t), :] * cw[CONV_WIDTH - 1 - d:CONV_WIDTH - d]
    xs_ref[0:pad, :] = xs_ref[t:t + pad, :]

    half_gates = jnp.dot(xc.astype(BF16), w_ref[0], preferred_element_type=F32) + b_ref[0]
    tanh_r = jnp.tanh(half_gates[:, :LRU_BLOCK_DIM])
    i = 0.5 * jnp.tanh(half_gates[:, LRU_BLOCK_DIM:]) + 0.5
    half_rate = (-0.5 * LRU_C) * jax.nn.softplus(-lam_ref[...])
    log_a = half_rate * tanh_r + half_rate
    a = jnp.exp(log_a)
    w = -jnp.tanh(log_a) * (1.0 + a * a)
    u = w * lax.rsqrt(jnp.maximum(w, 1e-30)) * i * xc
    pitch = _segment_pitch(seg)
    for s in range(nseg):
        a_ref[s * pitch:s * pitch + seg, :] = a[s * seg:(s + 1) * seg]
        u_ref[s * pitch:s * pitch + seg, :] = u[s * seg:(s + 1) * seg]

    hs = jnp.zeros((nseg, LRU_BLOCK_DIM), F32)
    ps = jnp.ones((nseg, LRU_BLOCK_DIM), F32)
    for j in range(seg):
        rows = pl.ds(j, nseg, stride=pitch)
        aj = a_ref[rows, :]
        hs = aj * hs + u_ref[rows, :]
        ps = aj * ps
        u_ref[rows, :] = hs
        a_ref[rows, :] = ps
    c = carry_ref[...]
    h = []
    for s in range(nseg):
        h.append(u_ref[s * pitch:s * pitch + seg, :] + a_ref[s * pitch:s * pitch + seg, :] * c)
        c = ps[s:s + 1] * c + hs[s:s + 1]
    carry_ref[...] = c
    h = jnp.concatenate(h, axis=0)
    o_ref[...] = (h * jax.nn.gelu(y_ref[...].astype(F32))).astype(o_ref.dtype)


def _segment_pitch(seg):
    pitch = -(-seg // 4) * 4
    while (pitch // 4) % 2 == 0:
        pitch += 4
    return pitch


def _rglru(z, conv_w, conv_b, wa, ba, wx, bx, lam, batch, seq):
    t = 2048
    nb = seq // t
    cwid = LRU_BLOCK_DIM
    scan_rows = SUBLANES * _segment_pitch(t // SUBLANES)
    xb = OFF_LX // cwid
    yb = OFF_LY // cwid
    w_cat = (0.5 * jnp.concatenate([wa, wx], axis=-1)).astype(BF16)
    b_cat = 0.5 * jnp.concatenate([ba.reshape(LRU_BLOCKS, 1, cwid),
                                   bx.reshape(LRU_BLOCKS, 1, cwid)], axis=-1)
    return pl.pallas_call(
        functools.partial(_lru_kernel, t=t),
        grid=(batch, LRU_BLOCKS, nb),
        in_specs=[
            pl.BlockSpec((t, cwid), lambda b, c, n: (b * nb + n, xb + c)),
            pl.BlockSpec((t, cwid), lambda b, c, n: (b * nb + n, yb + c)),
            pl.BlockSpec((CONV_WIDTH, cwid), lambda b, c, n: (0, c)),
            pl.BlockSpec((1, cwid), lambda b, c, n: (0, c)),
            pl.BlockSpec((1, cwid, 2 * cwid), lambda b, c, n: (c, 0, 0)),
            pl.BlockSpec((1, 1, 2 * cwid), lambda b, c, n: (c, 0, 0)),
            pl.BlockSpec((1, cwid), lambda b, c, n: (0, c)),
        ],
        out_specs=pl.BlockSpec((t, cwid), lambda b, c, n: (b * nb + n, c)),
        out_shape=jax.ShapeDtypeStruct((batch * seq, LRU_WIDTH), BF16),
        scratch_shapes=[pltpu.VMEM((SUBLANES + t, cwid), F32),
                        pltpu.VMEM((scan_rows, cwid), F32), pltpu.VMEM((scan_rows, cwid), F32),
                        pltpu.VMEM((1, cwid), F32)],
        compiler_params=_params(("parallel", "parallel", "arbitrary")),
        name="rglru",
    )(z, z, conv_w, conv_b[None, :], w_cat, b_cat, lam[None, :])


def _merge_kernel(a0_ref, a1_ref, a2_ref, w_ref, g0_ref, g1_ref, g2_ref, o_ref):
    acc = None
    for k, (a_ref, g_ref) in enumerate(((a0_ref, g0_ref), (a1_ref, g1_ref), (a2_ref, g2_ref))):
        d = jnp.dot(a_ref[...], w_ref[k], preferred_element_type=F32)
        term = _sigmoid(g_ref[...].astype(F32)) * d
        acc = term if acc is None else acc + term
    o_ref[...] = acc.astype(o_ref.dtype)


def _merge(o_ret, o_swa, o_lru, w_branch, layer, z):
    m = o_ret.shape[0]
    tm, tn = 1024, 512
    gb = OFF_GATE // tn
    gstep = D_MODEL // tn
    a_spec = pl.BlockSpec((tm, BRANCH_WIDTH), lambda i, j: (i, 0))

    def gate_spec(k):
        return pl.BlockSpec((tm, tn), lambda i, j: (i, gb + k * gstep + j))

    return pl.pallas_call(
        _merge_kernel,
        grid=(m // tm, D_MODEL // tn),
        in_specs=[a_spec, a_spec, a_spec,
                  pl.BlockSpec((None, N_BRANCH, BRANCH_WIDTH, tn), lambda i, j: (layer, 0, 0, j)),
                  gate_spec(0), gate_spec(1), gate_spec(2)],
        out_specs=pl.BlockSpec((tm, tn), lambda i, j: (i, j)),
        out_shape=jax.ShapeDtypeStruct((m, D_MODEL), BF16),
        compiler_params=_params(("parallel", "parallel")),
        name="merge",
    )(o_ret, o_swa, o_lru, w_branch, z, z, z)


def kernel(x, positions, norm_mix, w_in, swa_q_gain, swa_k_gain, swa_sinks, conv_w, conv_b,
           lru_wa, lru_ba, lru_wx, lru_bx, lru_lambda, w_branch, w_out, norm_mlp,
           w_mlp_in, w_mlp_out):
    batch, seq, d = x.shape
    m = batch * seq
    xf = x.reshape(m, d)
    rc, rs, sc, ss = _rope_tables(positions.reshape(m, 1))
    w_branch, w_out, w_mlp_out = (w.astype(BF16) for w in (w_branch, w_out, w_mlp_out))
    for l in range(DEPTH):
        h = _rmsnorm(xf, norm_mix[l])
        z = _matmul(h, w_in, l, tm=2048, tn=512, single_buffer_a=True, name="in_proj")
        o_ret = _retention(z, rc, rs, batch, seq)
        o_swa = _swa(z, sc, ss, swa_q_gain[l], swa_k_gain[l], swa_sinks[l], batch, seq)
        o_lru = _rglru(z, conv_w[l], conv_b[l], lru_wa[l], lru_ba[l], lru_wx[l], lru_bx[l],
                       lru_lambda[l], batch, seq)
        mixed = _merge(o_ret, o_swa, o_lru, w_branch, l, z)
        xf = _matmul(mixed, w_out, l, epilogue="residual", residual=xf, name="out_proj")
        h = _rmsnorm(xf, norm_mlp[l])
        u = _matmul(h, w_mlp_in, l, epilogue="relu2", tm=2048, tn=512, single_buffer_a=True,
                    name="mlp_in")
        xf = _matmul(u, w_mlp_out, l, epilogue="residual", residual=xf, tk=2048,
                     name="mlp_out")
    return xf.reshape(batch, seq, d)
```

```python
import functools

import jax
import jax.numpy as jnp
from jax import lax
from jax.experimental import pallas as pl
from jax.experimental.pallas import tpu as pltpu

F32 = jnp.float32
BF16 = jnp.bfloat16

D_MODEL = 4096
DEPTH = 2
HEAD_DIM = 128
RET_HEADS = D_MODEL // 512
RET_DK = HEAD_DIM
RET_DV = 2 * HEAD_DIM
RET_THETA = 10000.0
SWA_Q_HEADS = D_MODEL // 256
SWA_KV_HEADS = SWA_Q_HEADS // 4
SWA_GROUP = SWA_Q_HEADS // SWA_KV_HEADS
SWA_WINDOW = 128
ROPE_THETA = 500000.0
ROPE_DIM = HEAD_DIM // 4
LRU_WIDTH = D_MODEL // 2
LRU_BLOCKS = 16
LRU_BLOCK_DIM = LRU_WIDTH // LRU_BLOCKS
CONV_WIDTH = 4
LRU_C = 8.0
N_BRANCH = 3
BRANCH_WIDTH = D_MODEL // 2
D_FF = 4 * D_MODEL
EPS = 1e-6

OFF_RQ = 0
OFF_RK = OFF_RQ + RET_HEADS * RET_DK
OFF_RV = OFF_RK + RET_HEADS * RET_DK
OFF_RG = OFF_RV + RET_HEADS * RET_DV
OFF_SQ = OFF_RG + RET_HEADS * RET_DV
OFF_SK = OFF_SQ + SWA_Q_HEADS * HEAD_DIM
OFF_SV = OFF_SK + SWA_KV_HEADS * HEAD_DIM
OFF_LX = OFF_SV + SWA_KV_HEADS * HEAD_DIM
OFF_LY = OFF_LX + LRU_WIDTH
OFF_GATE = OFF_LY + LRU_WIDTH
IN_TOTAL = OFF_GATE + N_BRANCH * D_MODEL

LANES = 128
SUBLANES = 8
VMEM_LIMIT = 56 * 1024 * 1024


def _params(semantics, vmem=VMEM_LIMIT):
    return pltpu.CompilerParams(dimension_semantics=semantics, vmem_limit_bytes=vmem)


def _sigmoid(x):
    return 0.5 * jnp.tanh(0.5 * x) + 0.5


def _rope_tables_kernel(pos_ref, fr_ref, sr_ref, fs_ref, rc_ref, rs_ref, sc_ref, ss_ref):
    pos = pos_ref[...].astype(F32)
    ang_r = pos * fr_ref[...]
    rc_ref[...] = jnp.cos(ang_r)
    rs_ref[...] = jnp.sin(ang_r) * sr_ref[...]
    ang_s = pos * fs_ref[...]
    sc_ref[...] = jnp.cos(ang_s)
    ss_ref[...] = jnp.sin(ang_s)


def _rope_tables(pos):
    m = pos.shape[0]
    t = 1024
    half_r = RET_DK // 2
    f_r = RET_THETA ** (-jnp.arange(half_r, dtype=F32) / half_r)
    fr = jnp.concatenate([f_r, f_r])[None, :]
    sr = jnp.concatenate([-jnp.ones((half_r,), F32), jnp.ones((half_r,), F32)])[None, :]
    half_s = ROPE_DIM // 2
    f_s = ROPE_THETA ** (-jnp.arange(half_s, dtype=F32) / half_s)
    zeros_rest = jnp.zeros((HEAD_DIM - ROPE_DIM,), F32)
    fs = jnp.concatenate([f_s, f_s, zeros_rest])[None, :]
    row = pl.BlockSpec((1, LANES), lambda i: (0, 0))
    tab = pl.BlockSpec((t, LANES), lambda i: (i, 0))
    return pl.pallas_call(
        _rope_tables_kernel,
        grid=(m // t,),
        in_specs=[pl.BlockSpec((t, 1), lambda i: (i, 0)), row, row, row],
        out_specs=[tab] * 4,
        out_shape=[jax.ShapeDtypeStruct((m, LANES), F32)] * 4,
        compiler_params=_params(("parallel",)),
        name="rope_tables",
    )(pos, fr, sr, fs)


def _rmsnorm_kernel(x_ref, g_ref, o_ref):
    x = x_ref[...]
    ms = jnp.mean(x * x, axis=-1, keepdims=True)
    o_ref[...] = (x * lax.rsqrt(ms + EPS) * g_ref[...]).astype(o_ref.dtype)


def _rmsnorm(x, gain):
    m, d = x.shape
    t = 256
    return pl.pallas_call(
        _rmsnorm_kernel,
        grid=(m // t,),
        in_specs=[pl.BlockSpec((t, d), lambda i: (i, 0)),
                  pl.BlockSpec((1, d), lambda i: (0, 0))],
        out_specs=pl.BlockSpec((t, d), lambda i: (i, 0)),
        out_shape=jax.ShapeDtypeStruct((m, d), BF16),
        compiler_params=_params(("parallel",)),
        name="rmsnorm",
    )(x, gain[None, :])


def _mm_kernel(a_ref, b_ref, *rest, nk, epilogue):
    if epilogue == "residual":
        r_ref, o_ref = rest
        if nk == 1:
            o_ref[...] = r_ref[...] + jnp.dot(a_ref[...], b_ref[...].astype(BF16), preferred_element_type=F32)
        else:
            @pl.when(pl.program_id(2) == 0)
            def _():
                o_ref[...] = r_ref[...]

            o_ref[...] += jnp.dot(a_ref[...], b_ref[...].astype(BF16), preferred_element_type=F32)
    else:
        (o_ref,) = rest
        d = jnp.dot(a_ref[...], b_ref[...].astype(BF16), preferred_element_type=F32)
        if epilogue == "relu2":
            d = jnp.maximum(d, 0.0)
            d = d * d
        o_ref[...] = d.astype(o_ref.dtype)


def _matmul(a, b, layer, *, epilogue="cast", residual=None, tm=1024, tn=1024, tk=None,
            single_buffer_a=False, name="matmul"):
    m, kdim = a.shape
    n = b.shape[2]
    tk = kdim if tk is None else tk
    nk = kdim // tk
    assert m % tm == 0 and n % tn == 0 and kdim % tk == 0
    assert nk == 1 or epilogue == "residual"
    a_mode = dict(pipeline_mode=pl.Buffered(1)) if single_buffer_a else {}
    in_specs = [pl.BlockSpec((tm, tk), lambda i, j, k: (i, k), **a_mode),
                pl.BlockSpec((None, tk, tn), lambda i, j, k: (layer, k, j))]
    args = [a, b]
    if epilogue == "residual":
        in_specs.append(pl.BlockSpec((tm, tn), lambda i, j, k: (i, j)))
        args.append(residual)
        out_dtype = F32
    else:
        out_dtype = BF16
    return pl.pallas_call(
        functools.partial(_mm_kernel, nk=nk, epilogue=epilogue),
        grid=(m // tm, n // tn, nk),
        in_specs=in_specs,
        out_specs=pl.BlockSpec((tm, tn), lambda i, j, k: (i, j)),
        out_shape=jax.ShapeDtypeStruct((m, n), out_dtype),
        compiler_params=_params(("parallel", "parallel", "arbitrary")),
        name=name,
    )(*args)


def _ret_kernel(q_ref, k_ref, v_ref, g_ref, rc_ref, rs_ref, lg_ref, o_ref,
                state_ref, dmat_ref, *, t):
    n = pl.program_id(2)
    lg = lg_ref[0][:, :1]
    ii = lax.broadcasted_iota(jnp.int32, (t, 1), 0).astype(F32)

    @pl.when(n == 0)
    def _():
        state_ref[...] = jnp.zeros_like(state_ref)
        r = lax.broadcasted_iota(jnp.int32, (t, t), 0)
        c = lax.broadcasted_iota(jnp.int32, (t, t), 1)
        diff = (r - c).astype(F32)
        dmat_ref[...] = jnp.where(diff >= 0.0, jnp.exp(lg * jnp.maximum(diff, 0.0)), 0.0)

    rc = rc_ref[...]
    rs = rs_ref[...]
    q = q_ref[...].astype(F32)
    k = k_ref[...].astype(F32)
    qr = q * rc + pltpu.roll(q, RET_DK // 2, 1) * rs
    kr = (k * rc + pltpu.roll(k, RET_DK // 2, 1) * rs) * (RET_DK ** -0.5)
    v = v_ref[...]
    s = lax.dot_general(qr.astype(BF16), kr.astype(BF16), (((1,), (1,)), ((), ())),
                        preferred_element_type=F32)
    p = (s * dmat_ref[...]).astype(BF16)
    intra = jnp.dot(p, v, preferred_element_type=F32)
    xi = jnp.exp(lg * (ii + 1.0))
    st = state_ref[...]
    cross = jnp.dot((qr * xi).astype(BF16), st.astype(BF16), preferred_element_type=F32)
    zeta = jnp.exp(lg * (t - 1.0 - ii))
    kz = (kr * zeta).astype(BF16)
    kv = lax.dot_general(kz, v, (((0,), (0,)), ((), ())), preferred_element_type=F32)
    state_ref[...] = jnp.exp(lg * float(t)) * st + kv
    y = intra + cross
    y = y * lax.rsqrt(jnp.mean(y * y, axis=-1, keepdims=True) + EPS)
    g = g_ref[...].astype(F32)
    o_ref[...] = (g * _sigmoid(g) * y).astype(o_ref.dtype)


def _retention(z, rc, rs, batch, seq):
    t = 512
    nb = seq // t
    log_g = jnp.log1p(-jnp.exp2(-5.0 - jnp.arange(RET_HEADS, dtype=F32)))
    lg = jnp.broadcast_to(log_g[:, None, None], (RET_HEADS, 1, LANES))
    kb = OFF_RK // RET_DK
    vb = OFF_RV // RET_DV
    gb = OFF_RG // RET_DV
    return pl.pallas_call(
        functools.partial(_ret_kernel, t=t),
        grid=(batch, RET_HEADS, nb),
        in_specs=[
            pl.BlockSpec((t, RET_DK), lambda b, h, n: (b * nb + n, h)),
            pl.BlockSpec((t, RET_DK), lambda b, h, n: (b * nb + n, kb + h)),
            pl.BlockSpec((t, RET_DV), lambda b, h, n: (b * nb + n, vb + h)),
            pl.BlockSpec((t, RET_DV), lambda b, h, n: (b * nb + n, gb + h)),
            pl.BlockSpec((t, LANES), lambda b, h, n: (b * nb + n, 0)),
            pl.BlockSpec((t, LANES), lambda b, h, n: (b * nb + n, 0)),
            pl.BlockSpec((1, 1, LANES), lambda b, h, n: (h, 0, 0)),
        ],
        out_specs=pl.BlockSpec((t, RET_DV), lambda b, h, n: (b * nb + n, h)),
        out_shape=jax.ShapeDtypeStruct((batch * seq, RET_HEADS * RET_DV), BF16),
        scratch_shapes=[pltpu.VMEM((RET_DK, RET_DV), F32), pltpu.VMEM((t, t), F32)],
        compiler_params=_params(("parallel", "parallel", "arbitrary")),
        name="retention",
    )(z, z, z, z, rc, rs, lg)


def _norm_rope(x, gain, c, s, ones, rot):
    xf = x.astype(F32)
    ss = jnp.dot((xf * xf).astype(BF16), ones, preferred_element_type=F32)
    xn = xf * lax.rsqrt(ss * (1.0 / HEAD_DIM) + EPS) * gain
    partner = jnp.dot(xn.astype(BF16), rot, preferred_element_type=F32)
    return xn * c + partner * s


def _swa_kernel(q_ref, kc_ref, kp_ref, vc_ref, vp_ref, c_ref, s_ref, cp_ref, sp_ref,
                qg_ref, kg_ref, sink_ref, rot_ref, o_ref, *, tq):
    n = pl.program_id(2)
    w = SWA_WINDOW
    group_rows = SWA_GROUP * w
    ones = jnp.ones((HEAD_DIM, HEAD_DIM), BF16)
    rot = rot_ref[...]
    qg = qg_ref[...] * (HEAD_DIM ** -0.5)
    kg = kg_ref[...]
    c = c_ref[...]
    s = s_ref[...]
    k_prev = _norm_rope(kp_ref[...], kg, cp_ref[...], sp_ref[...], ones, rot)
    k_cur = _norm_rope(kc_ref[...], kg, c, s, ones, rot)
    k_all = jnp.concatenate([k_prev, k_cur], axis=0).astype(BF16)
    v_all = jnp.concatenate([vp_ref[...], vc_ref[...]], axis=0)
    v_t = v_all.astype(F32).T.astype(BF16)
    ones_t = jnp.ones((HEAD_DIM, 2 * w), BF16)
    sink = sink_ref[0]
    kj = lax.broadcasted_iota(jnp.int32, (2 * w, w), 0)
    qi = lax.broadcasted_iota(jnp.int32, (2 * w, w), 1)
    band = (kj > qi) & (kj <= qi + w)
    bias = jnp.concatenate([jnp.where(band, 0.0, -jnp.inf)] * SWA_GROUP, axis=1)
    bias_first = jnp.concatenate(
        [jnp.where(band & (kj >= jnp.where(n > 0, 0, w)), 0.0, -jnp.inf)] * SWA_GROUP, axis=1)
    nt = tq // w
    xq = jnp.concatenate([q_ref[t * w:(t + 1) * w, g * HEAD_DIM:(g + 1) * HEAD_DIM]
                          for t in range(nt) for g in range(SWA_GROUP)], axis=0)
    c4 = jnp.concatenate([c[t * w:(t + 1) * w] for t in range(nt)
                          for _ in range(SWA_GROUP)], axis=0)
    s4 = jnp.concatenate([s[t * w:(t + 1) * w] for t in range(nt)
                          for _ in range(SWA_GROUP)], axis=0)
    qs = _norm_rope(xq, qg, c4, s4, ones, rot).astype(BF16)
    sc = [lax.dot_general(k_all[t * w:(t + 2) * w], qs[t * group_rows:(t + 1) * group_rows],
                          (((1,), (1,)), ((), ())), preferred_element_type=F32)
          + (bias_first if t == 0 else bias) for t in range(nt)]
    m = [jnp.maximum(jnp.max(x, axis=0, keepdims=True), sink) for x in sc]
    p = [jnp.exp(x - mx).astype(BF16) for x, mx in zip(sc, m)]
    o_ext = [jnp.dot(jnp.concatenate([v_t[:, t * w:(t + 2) * w], ones_t], axis=0), p[t],
                     preferred_element_type=F32) for t in range(nt)]
    for t in range(nt):
        denom = o_ext[t][HEAD_DIM:HEAD_DIM + 1] + jnp.exp(sink - m[t])
        o_t = o_ext[t][:HEAD_DIM] / denom
        for g in range(SWA_GROUP):
            o_ref[t * w:(t + 1) * w, g * HEAD_DIM:(g + 1) * HEAD_DIM] = (
                o_t[:, g * w:(g + 1) * w].T.astype(o_ref.dtype))


def _rotate_half_matrix():
    half = ROPE_DIM // 2
    src = jnp.arange(HEAD_DIM)[:, None]
    dst = jnp.arange(HEAD_DIM)[None, :]
    minus = (dst < half) & (src == dst + half)
    plus = (dst >= half) & (dst < ROPE_DIM) & (src == dst - half)
    return (plus.astype(F32) - minus.astype(F32)).astype(BF16)


def _swa(z, sc, ss, q_gain, k_gain, sinks, batch, seq):
    tq = 1024
    w = SWA_WINDOW
    nb = seq // tq
    r = tq // w
    gw = SWA_GROUP * HEAD_DIM
    qb = OFF_SQ // gw
    kb = OFF_SK // HEAD_DIM
    vb = OFF_SV // HEAD_DIM
    sink_b = jnp.repeat(sinks.astype(F32).reshape(SWA_KV_HEADS, SWA_GROUP), w,
                        axis=1)[:, None, :]

    def cur(b, h, n):
        return b * nb + n

    def prev(b, h, n):
        return b * nb * r + jnp.maximum(n * r - 1, 0)

    return pl.pallas_call(
        functools.partial(_swa_kernel, tq=tq),
        grid=(batch, SWA_KV_HEADS, nb),
        in_specs=[
            pl.BlockSpec((tq, gw), lambda b, h, n: (cur(b, h, n), qb + h)),
            pl.BlockSpec((tq, HEAD_DIM), lambda b, h, n: (cur(b, h, n), kb + h)),
            pl.BlockSpec((w, HEAD_DIM), lambda b, h, n: (prev(b, h, n), kb + h)),
            pl.BlockSpec((tq, HEAD_DIM), lambda b, h, n: (cur(b, h, n), vb + h)),
            pl.BlockSpec((w, HEAD_DIM), lambda b, h, n: (prev(b, h, n), vb + h)),
            pl.BlockSpec((tq, LANES), lambda b, h, n: (cur(b, h, n), 0)),
            pl.BlockSpec((tq, LANES), lambda b, h, n: (cur(b, h, n), 0)),
            pl.BlockSpec((w, LANES), lambda b, h, n: (prev(b, h, n), 0)),
            pl.BlockSpec((w, LANES), lambda b, h, n: (prev(b, h, n), 0)),
            pl.BlockSpec((1, HEAD_DIM), lambda b, h, n: (0, 0)),
            pl.BlockSpec((1, HEAD_DIM), lambda b, h, n: (0, 0)),
            pl.BlockSpec((1, 1, SWA_GROUP * w), lambda b, h, n: (h, 0, 0)),
            pl.BlockSpec((HEAD_DIM, HEAD_DIM), lambda b, h, n: (0, 0)),
        ],
        out_specs=pl.BlockSpec((tq, gw), lambda b, h, n: (cur(b, h, n), h)),
        out_shape=jax.ShapeDtypeStruct((batch * seq, SWA_Q_HEADS * HEAD_DIM), BF16),
        compiler_params=_params(("parallel", "parallel", "parallel")),
        name="swa",
    )(z, z, z, z, z, sc, ss, sc, ss, q_gain[None, :], k_gain[None, :], sink_b,
      _rotate_half_matrix())


def _lru_kernel(x_ref, y_ref, cw_ref, cb_ref, w_ref, b_ref, lam_ref, o_ref,
                xs_ref, a_ref, u_ref, carry_ref, *, t):
    n = pl.program_id(2)
    pad = SUBLANES
    nseg = SUBLANES
    seg = t // nseg

    @pl.when(n == 0)
    def _():
        xs_ref[0:pad, :] = jnp.zeros((pad, LRU_BLOCK_DIM), F32)
        carry_ref[...] = jnp.zeros_like(carry_ref)

    xs_ref[pad:pad + t, :] = x_ref[...].astype(F32)
    cw = cw_ref[...]
    xc = cb_ref[...] + xs_ref[pad:pad + t, :] * cw[CONV_WIDTH - 1:CONV_WIDTH]
    for d in range(1, CONV_WIDTH):
        xc = xc + xs_ref[pl.ds(pad - d, t), :] * cw[CONV_WIDTH - 1 - d:CONV_WIDTH - d]
    xs_ref[0:pad, :] = xs_ref[t:t + pad, :]

    half_gates = jnp.dot(xc.astype(BF16), w_ref[0], preferred_element_type=F32) + b_ref[0]
    tanh_r = jnp.tanh(half_gates[:, :LRU_BLOCK_DIM])
    i = 0.5 * jnp.tanh(half_gates[:, LRU_BLOCK_DIM:]) + 0.5
    half_rate = (-0.5 * LRU_C) * jax.nn.softplus(-lam_ref[...])
    log_a = half_rate * tanh_r + half_rate
    a = jnp.exp(log_a)
    w = -jnp.tanh(log_a) * (1.0 + a * a)
    u = w * lax.rsqrt(jnp.maximum(w, 1e-30)) * i * xc
    pitch = _segment_pitch(seg)
    for s in range(nseg):
        a_ref[s * pitch:s * pitch + seg, :] = a[s * seg:(s + 1) * seg]
        u_ref[s * pitch:s * pitch + seg, :] = u[s * seg:(s + 1) * seg]

    hs = jnp.zeros((nseg, LRU_BLOCK_DIM), F32)
    ps = jnp.ones((nseg, LRU_BLOCK_DIM), F32)
    for j in range(seg):
        rows = pl.ds(j, nseg, stride=pitch)
        aj = a_ref[rows, :]
        hs = aj * hs + u_ref[rows, :]
        ps = aj * ps
        u_ref[rows, :] = hs
        a_ref[rows, :] = ps
    c = carry_ref[...]
    h = []
    for s in range(nseg):
        h.append(u_ref[s * pitch:s * pitch + seg, :] + a_ref[s * pitch:s * pitch + seg, :] * c)
        c = ps[s:s + 1] * c + hs[s:s + 1]
    carry_ref[...] = c
    h = jnp.concatenate(h, axis=0)
    o_ref[...] = (h * jax.nn.gelu(y_ref[...].astype(F32))).astype(o_ref.dtype)


def _segment_pitch(seg):
    pitch = -(-seg // 4) * 4
    while (pitch // 4) % 2 == 0:
        pitch += 4
    return pitch


def _rglru(z, conv_w, conv_b, wa, ba, wx, bx, lam, batch, seq):
    t = 2048
    nb = seq // t
    cwid = LRU_BLOCK_DIM
    scan_rows = SUBLANES * _segment_pitch(t // SUBLANES)
    xb = OFF_LX // cwid
    yb = OFF_LY // cwid
    w_cat = (0.5 * jnp.concatenate([wa, wx], axis=-1)).astype(BF16)
    b_cat = 0.5 * jnp.concatenate([ba.reshape(LRU_BLOCKS, 1, cwid),
                                   bx.reshape(LRU_BLOCKS, 1, cwid)], axis=-1)
    return pl.pallas_call(
        functools.partial(_lru_kernel, t=t),
        grid=(batch, LRU_BLOCKS, nb),
        in_specs=[
            pl.BlockSpec((t, cwid), lambda b, c, n: (b * nb + n, xb + c)),
            pl.BlockSpec((t, cwid), lambda b, c, n: (b * nb + n, yb + c)),
            pl.BlockSpec((CONV_WIDTH, cwid), lambda b, c, n: (0, c)),
            pl.BlockSpec((1, cwid), lambda b, c, n: (0, c)),
            pl.BlockSpec((1, cwid, 2 * cwid), lambda b, c, n: (c, 0, 0)),
            pl.BlockSpec((1, 1, 2 * cwid), lambda b, c, n: (c, 0, 0)),
            pl.BlockSpec((1, cwid), lambda b, c, n: (0, c)),
        ],
        out_specs=pl.BlockSpec((t, cwid), lambda b, c, n: (b * nb + n, c)),
        out_shape=jax.ShapeDtypeStruct((batch * seq, LRU_WIDTH), BF16),
        scratch_shapes=[pltpu.VMEM((SUBLANES + t, cwid), F32),
                        pltpu.VMEM((scan_rows, cwid), F32), pltpu.VMEM((scan_rows, cwid), F32),
                        pltpu.VMEM((1, cwid), F32)],
        compiler_params=_params(("parallel", "parallel", "arbitrary")),
        name="rglru",
    )(z, z, conv_w, conv_b[None, :], w_cat, b_cat, lam[None, :])


def _merge_kernel(a0_ref, a1_ref, a2_ref, w_ref, g0_ref, g1_ref, g2_ref, o_ref):
    acc = None
    for k, (a_ref, g_ref) in enumerate(((a0_ref, g0_ref), (a1_ref, g1_ref), (a2_ref, g2_ref))):
        d = jnp.dot(a_ref[...], w_ref[k], preferred_element_type=F32)
        term = _sigmoid(g_ref[...].astype(F32)) * d
        acc = term if acc is None else acc + term
    o_ref[...] = acc.astype(o_ref.dtype)


def _merge(o_ret, o_swa, o_lru, w_branch, layer, z):
    m = o_ret.shape[0]
    tm, tn = 1024, 512
    gb = OFF_GATE // tn
    gstep = D_MODEL // tn
    a_spec = pl.BlockSpec((tm, BRANCH_WIDTH), lambda i, j: (i, 0))

    def gate_spec(k):
        return pl.BlockSpec((tm, tn), lambda i, j: (i, gb + k * gstep + j))

    return pl.pallas_call(
        _merge_kernel,
        grid=(m // tm, D_MODEL // tn),
        in_specs=[a_spec, a_spec, a_spec,
                  pl.BlockSpec((None, N_BRANCH, BRANCH_WIDTH, tn), lambda i, j: (layer, 0, 0, j)),
                  gate_spec(0), gate_spec(1), gate_spec(2)],
        out_specs=pl.BlockSpec((tm, tn), lambda i, j: (i, j)),
        out_shape=jax.ShapeDtypeStruct((m, D_MODEL), BF16),
        compiler_params=_params(("parallel", "parallel")),
        name="merge",
    )(o_ret, o_swa, o_lru, w_branch, z, z, z)


def kernel(x, positions, norm_mix, w_in, swa_q_gain, swa_k_gain, swa_sinks, conv_w, conv_b,
           lru_wa, lru_ba, lru_wx, lru_bx, lru_lambda, w_branch, w_out, norm_mlp,
           w_mlp_in, w_mlp_out):
    batch, seq, d = x.shape
    m = batch * seq
    xf = x.reshape(m, d)
    rc, rs, sc, ss = _rope_tables(positions.reshape(m, 1))
    w_branch, w_out, w_mlp_out = (w.astype(BF16) for w in (w_branch, w_out, w_mlp_out))
    for l in range(DEPTH):
        h = _rmsnorm(xf, norm_mix[l])
        z = _matmul(h, w_in, l, tm=2048, tn=512, single_buffer_a=True, name="in_proj")
        o_ret = _retention(z, rc, rs, batch, seq)
        o_swa = _swa(z, sc, ss, swa_q_gain[l], swa_k_gain[l], swa_sinks[l], batch, seq)
        o_lru = _rglru(z, conv_w[l], conv_b[l], lru_wa[l], lru_ba[l], lru_wx[l], lru_bx[l],
                       lru_lambda[l], batch, seq)
        mixed = _merge(o_ret, o_swa, o_lru, w_branch, l, z)
        xf = _matmul(mixed, w_out, l, epilogue="residual", residual=xf, name="out_proj")
        h = _rmsnorm(xf, norm_mlp[l])
        u = _matmul(h, w_mlp_in, l, epilogue="relu2", tm=2048, tn=512, single_buffer_a=True,
                    name="mlp_in")
        xf = _matmul(u, w_mlp_out, l, epilogue="residual", residual=xf, tk=4096,
                     name="mlp_out")
    return xf.reshape(batch, seq, d)
```

```python
import functools

import jax
import jax.numpy as jnp
from jax import lax
from jax.experimental import pallas as pl
from jax.experimental.pallas import tpu as pltpu

F32 = jnp.float32
BF16 = jnp.bfloat16

D_MODEL = 4096
DEPTH = 2
HEAD_DIM = 128
RET_HEADS = D_MODEL // 512
RET_DK = HEAD_DIM
RET_DV = 2 * HEAD_DIM
RET_THETA = 10000.0
SWA_Q_HEADS = D_MODEL // 256
SWA_KV_HEADS = SWA_Q_HEADS // 4
SWA_GROUP = SWA_Q_HEADS // SWA_KV_HEADS
SWA_WINDOW = 128
ROPE_THETA = 500000.0
ROPE_DIM = HEAD_DIM // 4
LRU_WIDTH = D_MODEL // 2
LRU_BLOCKS = 16
LRU_BLOCK_DIM = LRU_WIDTH // LRU_BLOCKS
CONV_WIDTH = 4
LRU_C = 8.0
N_BRANCH = 3
BRANCH_WIDTH = D_MODEL // 2
D_FF = 4 * D_MODEL
EPS = 1e-6

OFF_RQ = 0
OFF_RK = OFF_RQ + RET_HEADS * RET_DK
OFF_RV = OFF_RK + RET_HEADS * RET_DK
OFF_RG = OFF_RV + RET_HEADS * RET_DV
OFF_SQ = OFF_RG + RET_HEADS * RET_DV
OFF_SK = OFF_SQ + SWA_Q_HEADS * HEAD_DIM
OFF_SV = OFF_SK + SWA_KV_HEADS * HEAD_DIM
OFF_LX = OFF_SV + SWA_KV_HEADS * HEAD_DIM
OFF_LY = OFF_LX + LRU_WIDTH
OFF_GATE = OFF_LY + LRU_WIDTH
IN_TOTAL = OFF_GATE + N_BRANCH * D_MODEL

LANES = 128
SUBLANES = 8
VMEM_LIMIT = 56 * 1024 * 1024


def _params(semantics, vmem=VMEM_LIMIT):
    return pltpu.CompilerParams(dimension_semantics=semantics, vmem_limit_bytes=vmem)


def _sigmoid(x):
    return 0.5 * jnp.tanh(0.5 * x) + 0.5


def _rope_tables_kernel(pos_ref, fr_ref, sr_ref, fs_ref, rc_ref, rs_ref, sc_ref, ss_ref):
    pos = pos_ref[...].astype(F32)
    ang_r = pos * fr_ref[...]
    rc_ref[...] = jnp.cos(ang_r)
    rs_ref[...] = jnp.sin(ang_r) * sr_ref[...]
    ang_s = pos * fs_ref[...]
    sc_ref[...] = jnp.cos(ang_s)
    ss_ref[...] = jnp.sin(ang_s)


def _rope_tables(pos):
    m = pos.shape[0]
    t = 1024
    half_r = RET_DK // 2
    f_r = RET_THETA ** (-jnp.arange(half_r, dtype=F32) / half_r)
    fr = jnp.concatenate([f_r, f_r])[None, :]
    sr = jnp.concatenate([-jnp.ones((half_r,), F32), jnp.ones((half_r,), F32)])[None, :]
    half_s = ROPE_DIM // 2
    f_s = ROPE_THETA ** (-jnp.arange(half_s, dtype=F32) / half_s)
    zeros_rest = jnp.zeros((HEAD_DIM - ROPE_DIM,), F32)
    fs = jnp.concatenate([f_s, f_s, zeros_rest])[None, :]
    row = pl.BlockSpec((1, LANES), lambda i: (0, 0))
    tab = pl.BlockSpec((t, LANES), lambda i: (i, 0))
    return pl.pallas_call(
        _rope_tables_kernel,
        grid=(m // t,),
        in_specs=[pl.BlockSpec((t, 1), lambda i: (i, 0)), row, row, row],
        out_specs=[tab] * 4,
        out_shape=[jax.ShapeDtypeStruct((m, LANES), F32)] * 4,
        compiler_params=_params(("parallel",)),
        name="rope_tables",
    )(pos, fr, sr, fs)


def _rmsnorm_kernel(x_ref, g_ref, o_ref):
    x = x_ref[...]
    ms = jnp.mean(x * x, axis=-1, keepdims=True)
    o_ref[...] = (x * lax.rsqrt(ms + EPS) * g_ref[...]).astype(o_ref.dtype)


def _rmsnorm(x, gain):
    m, d = x.shape
    t = 512
    return pl.pallas_call(
        _rmsnorm_kernel,
        grid=(m // t,),
        in_specs=[pl.BlockSpec((t, d), lambda i: (i, 0)),
                  pl.BlockSpec((1, d), lambda i: (0, 0))],
        out_specs=pl.BlockSpec((t, d), lambda i: (i, 0)),
        out_shape=jax.ShapeDtypeStruct((m, d), BF16),
        compiler_params=_params(("parallel",)),
        name="rmsnorm",
    )(x, gain[None, :])


def _mm_kernel(a_ref, b_ref, *rest, nk, epilogue):
    if epilogue == "residual":
        r_ref, o_ref = rest
        if nk == 1:
            o_ref[...] = r_ref[...] + jnp.dot(a_ref[...], b_ref[...].astype(BF16), preferred_element_type=F32)
        else:
            @pl.when(pl.program_id(2) == 0)
            def _():
                o_ref[...] = r_ref[...]

            o_ref[...] += jnp.dot(a_ref[...], b_ref[...].astype(BF16), preferred_element_type=F32)
    else:
        (o_ref,) = rest
        d = jnp.dot(a_ref[...], b_ref[...].astype(BF16), preferred_element_type=F32)
        if epilogue == "relu2":
            d = jnp.maximum(d, 0.0)
            d = d * d
        o_ref[...] = d.astype(o_ref.dtype)


def _matmul(a, b, layer, *, epilogue="cast", residual=None, tm=1024, tn=1024, tk=None,
            single_buffer_a=False, name="matmul"):
    m, kdim = a.shape
    n = b.shape[2]
    tk = kdim if tk is None else tk
    nk = kdim // tk
    assert m % tm == 0 and n % tn == 0 and kdim % tk == 0
    assert nk == 1 or epilogue == "residual"
    a_mode = dict(pipeline_mode=pl.Buffered(1)) if single_buffer_a else {}
    in_specs = [pl.BlockSpec((tm, tk), lambda i, j, k: (i, k), **a_mode),
                pl.BlockSpec((None, tk, tn), lambda i, j, k: (layer, k, j))]
    args = [a, b]
    if epilogue == "residual":
        in_specs.append(pl.BlockSpec((tm, tn), lambda i, j, k: (i, j)))
        args.append(residual)
        out_dtype = F32
    else:
        out_dtype = BF16
    return pl.pallas_call(
        functools.partial(_mm_kernel, nk=nk, epilogue=epilogue),
        grid=(m // tm, n // tn, nk),
        in_specs=in_specs,
        out_specs=pl.BlockSpec((tm, tn), lambda i, j, k: (i, j)),
        out_shape=jax.ShapeDtypeStruct((m, n), out_dtype),
        compiler_params=_params(("parallel", "parallel", "arbitrary")),
        name=name,
    )(*args)


def _ret_kernel(q_ref, k_ref, v_ref, g_ref, rc_ref, rs_ref, lg_ref, o_ref,
                state_ref, dmat_ref, *, t, hp):
    n = pl.program_id(2)
    ii = lax.broadcasted_iota(jnp.int32, (t, 1), 0).astype(F32)
    lgs = [lg_ref[i][:, :1] for i in range(hp)]

    @pl.when(n == 0)
    def _():
        state_ref[...] = jnp.zeros_like(state_ref)
        r = lax.broadcasted_iota(jnp.int32, (t, t), 0)
        c = lax.broadcasted_iota(jnp.int32, (t, t), 1)
        diff = (r - c).astype(F32)
        for i in range(hp):
            dmat_ref[i] = jnp.where(diff >= 0.0, jnp.exp(lgs[i] * jnp.maximum(diff, 0.0)), 0.0)

    rc = rc_ref[...]
    rs = rs_ref[...]
    for i in range(hp):
        lg = lgs[i]
        q = q_ref[:, i * RET_DK:(i + 1) * RET_DK].astype(F32)
        k = k_ref[:, i * RET_DK:(i + 1) * RET_DK].astype(F32)
        qr = q * rc + pltpu.roll(q, RET_DK // 2, 1) * rs
        kr = (k * rc + pltpu.roll(k, RET_DK // 2, 1) * rs) * (RET_DK ** -0.5)
        v = v_ref[:, i * RET_DV:(i + 1) * RET_DV]
        s = lax.dot_general(qr.astype(BF16), kr.astype(BF16), (((1,), (1,)), ((), ())),
                            preferred_element_type=F32)
        p = (s * dmat_ref[i]).astype(BF16)
        intra = jnp.dot(p, v, preferred_element_type=F32)
        xi = jnp.exp(lg * (ii + 1.0))
        st = state_ref[i]
        cross = jnp.dot((qr * xi).astype(BF16), st.astype(BF16), preferred_element_type=F32)
        zeta = jnp.exp(lg * (t - 1.0 - ii))
        kz = (kr * zeta).astype(BF16)
        kv = lax.dot_general(kz, v, (((0,), (0,)), ((), ())), preferred_element_type=F32)
        state_ref[i] = jnp.exp(lg * float(t)) * st + kv
        y = intra + cross
        y = y * lax.rsqrt(jnp.mean(y * y, axis=-1, keepdims=True) + EPS)
        g = g_ref[:, i * RET_DV:(i + 1) * RET_DV].astype(F32)
        o_ref[:, i * RET_DV:(i + 1) * RET_DV] = (g * _sigmoid(g) * y).astype(o_ref.dtype)


def _retention(z, rc, rs, batch, seq):
    t = 512
    hp = 4
    nb = seq // t
    log_g = jnp.log1p(-jnp.exp2(-5.0 - jnp.arange(RET_HEADS, dtype=F32)))
    lg = jnp.broadcast_to(log_g[:, None, None], (RET_HEADS, 1, LANES))
    kb = OFF_RK // (hp * RET_DK)
    vb = OFF_RV // (hp * RET_DV)
    gb = OFF_RG // (hp * RET_DV)
    return pl.pallas_call(
        functools.partial(_ret_kernel, t=t, hp=hp),
        grid=(batch, RET_HEADS // hp, nb),
        in_specs=[
            pl.BlockSpec((t, hp * RET_DK), lambda b, h, n: (b * nb + n, h)),
            pl.BlockSpec((t, hp * RET_DK), lambda b, h, n: (b * nb + n, kb + h)),
            pl.BlockSpec((t, hp * RET_DV), lambda b, h, n: (b * nb + n, vb + h)),
            pl.BlockSpec((t, hp * RET_DV), lambda b, h, n: (b * nb + n, gb + h)),
            pl.BlockSpec((t, LANES), lambda b, h, n: (b * nb + n, 0)),
            pl.BlockSpec((t, LANES), lambda b, h, n: (b * nb + n, 0)),
            pl.BlockSpec((hp, 1, LANES), lambda b, h, n: (h, 0, 0)),
        ],
        out_specs=pl.BlockSpec((t, hp * RET_DV), lambda b, h, n: (b * nb + n, h)),
        out_shape=jax.ShapeDtypeStruct((batch * seq, RET_HEADS * RET_DV), BF16),
        scratch_shapes=[pltpu.VMEM((hp, RET_DK, RET_DV), F32), pltpu.VMEM((hp, t, t), F32)],
        compiler_params=_params(("parallel", "parallel", "arbitrary")),
        name="retention",
    )(z, z, z, z, rc, rs, lg)


def _norm_rope(x, gain, c, s, ones, rot):
    xf = x.astype(F32)
    ss = jnp.dot((xf * xf).astype(BF16), ones, preferred_element_type=F32)
    xn = xf * lax.rsqrt(ss * (1.0 / HEAD_DIM) + EPS) * gain
    partner = jnp.dot(xn.astype(BF16), rot, preferred_element_type=F32)
    return xn * c + partner * s


def _swa_kernel(q_ref, kc_ref, kp_ref, vc_ref, vp_ref, c_ref, s_ref, cp_ref, sp_ref,
                qg_ref, kg_ref, sink_ref, rot_ref, o_ref, *, tq):
    n = pl.program_id(2)
    w = SWA_WINDOW
    group_rows = SWA_GROUP * w
    ones = jnp.ones((HEAD_DIM, HEAD_DIM), BF16)
    rot = rot_ref[...]
    qg = qg_ref[...] * (HEAD_DIM ** -0.5)
    kg = kg_ref[...]
    c = c_ref[...]
    s = s_ref[...]
    k_prev = _norm_rope(kp_ref[...], kg, cp_ref[...], sp_ref[...], ones, rot)
    k_cur = _norm_rope(kc_ref[...], kg, c, s, ones, rot)
    k_all = jnp.concatenate([k_prev, k_cur], axis=0).astype(BF16)
    v_all = jnp.concatenate([vp_ref[...], vc_ref[...]], axis=0)
    v_t = v_all.astype(F32).T.astype(BF16)
    ones_t = jnp.ones((HEAD_DIM, 2 * w), BF16)
    sink = sink_ref[0]
    kj = lax.broadcasted_iota(jnp.int32, (2 * w, w), 0)
    qi = lax.broadcasted_iota(jnp.int32, (2 * w, w), 1)
    band = (kj > qi) & (kj <= qi + w)
    bias = jnp.concatenate([jnp.where(band, 0.0, -jnp.inf)] * SWA_GROUP, axis=1)
    bias_first = jnp.concatenate(
        [jnp.where(band & (kj >= jnp.where(n > 0, 0, w)), 0.0, -jnp.inf)] * SWA_GROUP, axis=1)
    nt = tq // w
    xq = jnp.concatenate([q_ref[t * w:(t + 1) * w, g * HEAD_DIM:(g + 1) * HEAD_DIM]
                          for t in range(nt) for g in range(SWA_GROUP)], axis=0)
    c4 = jnp.concatenate([c[t * w:(t + 1) * w] for t in range(nt)
                          for _ in range(SWA_GROUP)], axis=0)
    s4 = jnp.concatenate([s[t * w:(t + 1) * w] for t in range(nt)
                          for _ in range(SWA_GROUP)], axis=0)
    qs = _norm_rope(xq, qg, c4, s4, ones, rot).astype(BF16)
    sc = [lax.dot_general(k_all[t * w:(t + 2) * w], qs[t * group_rows:(t + 1) * group_rows],
                          (((1,), (1,)), ((), ())), preferred_element_type=F32)
          + (bias_first if t == 0 else bias) for t in range(nt)]
    m = [jnp.maximum(jnp.max(x, axis=0, keepdims=True), sink) for x in sc]
    p = [jnp.exp(x - mx).astype(BF16) for x, mx in zip(sc, m)]
    o_ext = [jnp.dot(jnp.concatenate([v_t[:, t * w:(t + 2) * w], ones_t], axis=0), p[t],
                     preferred_element_type=F32) for t in range(nt)]
    for t in range(nt):
        denom = o_ext[t][HEAD_DIM:HEAD_DIM + 1] + jnp.exp(sink - m[t])
        o_t = o_ext[t][:HEAD_DIM] / denom
        for g in range(SWA_GROUP):
            o_ref[t * w:(t + 1) * w, g * HEAD_DIM:(g + 1) * HEAD_DIM] = (
                o_t[:, g * w:(g + 1) * w].T.astype(o_ref.dtype))


def _rotate_half_matrix():
    half = ROPE_DIM // 2
    src = jnp.arange(HEAD_DIM)[:, None]
    dst = jnp.arange(HEAD_DIM)[None, :]
    minus = (dst < half) & (src == dst + half)
    plus = (dst >= half) & (dst < ROPE_DIM) & (src == dst - half)
    return (plus.astype(F32) - minus.astype(F32)).astype(BF16)


def _swa(z, sc, ss, q_gain, k_gain, sinks, batch, seq):
    tq = 1024
    w = SWA_WINDOW
    nb = seq // tq
    r = tq // w
    gw = SWA_GROUP * HEAD_DIM
    qb = OFF_SQ // gw
    kb = OFF_SK // HEAD_DIM
    vb = OFF_SV // HEAD_DIM
    sink_b = jnp.repeat(sinks.astype(F32).reshape(SWA_KV_HEADS, SWA_GROUP), w,
                        axis=1)[:, None, :]

    def cur(b, h, n):
        return b * nb + n

    def prev(b, h, n):
        return b * nb * r + jnp.maximum(n * r - 1, 0)

    return pl.pallas_call(
        functools.partial(_swa_kernel, tq=tq),
        grid=(batch, SWA_KV_HEADS, nb),
        in_specs=[
            pl.BlockSpec((tq, gw), lambda b, h, n: (cur(b, h, n), qb + h)),
            pl.BlockSpec((tq, HEAD_DIM), lambda b, h, n: (cur(b, h, n), kb + h)),
            pl.BlockSpec((w, HEAD_DIM), lambda b, h, n: (prev(b, h, n), kb + h)),
            pl.BlockSpec((tq, HEAD_DIM), lambda b, h, n: (cur(b, h, n), vb + h)),
            pl.BlockSpec((w, HEAD_DIM), lambda b, h, n: (prev(b, h, n), vb + h)),
            pl.BlockSpec((tq, LANES), lambda b, h, n: (cur(b, h, n), 0)),
            pl.BlockSpec((tq, LANES), lambda b, h, n: (cur(b, h, n), 0)),
            pl.BlockSpec((w, LANES), lambda b, h, n: (prev(b, h, n), 0)),
            pl.BlockSpec((w, LANES), lambda b, h, n: (prev(b, h, n), 0)),
            pl.BlockSpec((1, HEAD_DIM), lambda b, h, n: (0, 0)),
            pl.BlockSpec((1, HEAD_DIM), lambda b, h, n: (0, 0)),
            pl.BlockSpec((1, 1, SWA_GROUP * w), lambda b, h, n: (h, 0, 0)),
            pl.BlockSpec((HEAD_DIM, HEAD_DIM), lambda b, h, n: (0, 0)),
        ],
        out_specs=pl.BlockSpec((tq, gw), lambda b, h, n: (cur(b, h, n), h)),
        out_shape=jax.ShapeDtypeStruct((batch * seq, SWA_Q_HEADS * HEAD_DIM), BF16),
        compiler_params=_params(("parallel", "parallel", "parallel")),
        name="swa",
    )(z, z, z, z, z, sc, ss, sc, ss, q_gain[None, :], k_gain[None, :], sink_b,
      _rotate_half_matrix())


def _lru_kernel(x_ref, y_ref, cw_ref, cb_ref, w_ref, b_ref, lam_ref, o_ref,
                xs_ref, a_ref, u_ref, carry_ref, *, t, nblk):
    n = pl.program_id(2)
    pad = SUBLANES
    nseg = SUBLANES
    seg = t // nseg
    pitch = _segment_pitch(seg)
    wd = LRU_BLOCK_DIM

    @pl.when(n == 0)
    def _():
        xs_ref[:, 0:pad, :] = jnp.zeros((nblk, pad, wd), F32)
        carry_ref[...] = jnp.zeros_like(carry_ref)

    for blk in range(nblk):
        lanes = slice(blk * wd, (blk + 1) * wd)
        xs = xs_ref.at[blk]
        xs[pad:pad + t, :] = x_ref[:, lanes].astype(F32)
        cw = cw_ref[:, lanes]
        xc = cb_ref[:, lanes] + xs[pad:pad + t, :] * cw[CONV_WIDTH - 1:CONV_WIDTH]
        for d in range(1, CONV_WIDTH):
            xc = xc + xs[pl.ds(pad - d, t), :] * cw[CONV_WIDTH - 1 - d:CONV_WIDTH - d]
        xs[0:pad, :] = xs[t:t + pad, :]

        half_gates = (jnp.dot(xc.astype(BF16), w_ref[blk], preferred_element_type=F32)
                      + b_ref[blk])
        tanh_r = jnp.tanh(half_gates[:, :wd])
        i = 0.5 * jnp.tanh(half_gates[:, wd:]) + 0.5
        half_rate = (-0.5 * LRU_C) * jax.nn.softplus(-lam_ref[:, lanes])
        a = jnp.exp(half_rate * tanh_r + half_rate)
        w = 1.0 - a * a
        u = w * lax.rsqrt(jnp.maximum(w, 1e-30)) * i * xc
        for s in range(nseg):
            a_ref[blk, s * pitch:s * pitch + seg, :] = a[s * seg:(s + 1) * seg]
            u_ref[blk, s * pitch:s * pitch + seg, :] = u[s * seg:(s + 1) * seg]

    hs = [jnp.zeros((nseg, wd), F32)] * nblk
    ps = [jnp.ones((nseg, wd), F32)] * nblk
    for j in range(seg):
        rows = pl.ds(j, nseg, stride=pitch)
        for blk in range(nblk):
            aj = a_ref[blk, rows, :]
            hs[blk] = aj * hs[blk] + u_ref[blk, rows, :]
            ps[blk] = aj * ps[blk]
            u_ref[blk, rows, :] = hs[blk]
            a_ref[blk, rows, :] = ps[blk]

    for blk in range(nblk):
        lanes = slice(blk * wd, (blk + 1) * wd)
        c = carry_ref[blk]
        h = []
        for s in range(nseg):
            h.append(u_ref[blk, s * pitch:s * pitch + seg, :]
                     + a_ref[blk, s * pitch:s * pitch + seg, :] * c)
            c = ps[blk][s:s + 1] * c + hs[blk][s:s + 1]
        carry_ref[blk] = c
        h = jnp.concatenate(h, axis=0)
        o_ref[:, lanes] = (h * jax.nn.gelu(y_ref[:, lanes].astype(F32))).astype(o_ref.dtype)


def _segment_pitch(seg):
    pitch = -(-seg // 4) * 4
    while (pitch // 4) % 2 == 0:
        pitch += 4
    return pitch


def _rglru(z, conv_w, conv_b, wa, ba, wx, bx, lam, batch, seq):
    t = 2048
    nb = seq // t
    cwid = LRU_BLOCK_DIM
    scan_rows = SUBLANES * _segment_pitch(t // SUBLANES)
    xb = OFF_LX // cwid
    yb = OFF_LY // cwid
    w_cat = (0.5 * jnp.concatenate([wa, wx], axis=-1)).astype(BF16)
    b_cat = 0.5 * jnp.concatenate([ba.reshape(LRU_BLOCKS, 1, cwid),
                                   bx.reshape(LRU_BLOCKS, 1, cwid)], axis=-1)
    nblk = 2
    cw = nblk * cwid
    xb //= nblk
    yb //= nblk
    return pl.pallas_call(
        functools.partial(_lru_kernel, t=t, nblk=nblk),
        grid=(batch, LRU_BLOCKS // nblk, nb),
        in_specs=[
            pl.BlockSpec((t, cw), lambda b, c, n: (b * nb + n, xb + c)),
            pl.BlockSpec((t, cw), lambda b, c, n: (b * nb + n, yb + c)),
            pl.BlockSpec((CONV_WIDTH, cw), lambda b, c, n: (0, c)),
            pl.BlockSpec((1, cw), lambda b, c, n: (0, c)),
            pl.BlockSpec((nblk, cwid, 2 * cwid), lambda b, c, n: (c, 0, 0)),
            pl.BlockSpec((nblk, 1, 2 * cwid), lambda b, c, n: (c, 0, 0)),
            pl.BlockSpec((1, cw), lambda b, c, n: (0, c)),
        ],
        out_specs=pl.BlockSpec((t, cw), lambda b, c, n: (b * nb + n, c)),
        out_shape=jax.ShapeDtypeStruct((batch * seq, LRU_WIDTH), BF16),
        scratch_shapes=[pltpu.VMEM((nblk, SUBLANES + t, cwid), F32),
                        pltpu.VMEM((nblk, scan_rows, cwid), F32),
                        pltpu.VMEM((nblk, scan_rows, cwid), F32),
                        pltpu.VMEM((nblk, 1, cwid), F32)],
        compiler_params=_params(("parallel", "parallel", "arbitrary")),
        name="rglru",
    )(z, z, conv_w, conv_b[None, :], w_cat, b_cat, lam[None, :])


def _merge_kernel(a0_ref, a1_ref, a2_ref, w_ref, g0_ref, g1_ref, g2_ref, o_ref):
    acc = None
    for k, (a_ref, g_ref) in enumerate(((a0_ref, g0_ref), (a1_ref, g1_ref), (a2_ref, g2_ref))):
        d = jnp.dot(a_ref[...], w_ref[k], preferred_element_type=F32)
        term = _sigmoid(g_ref[...].astype(F32)) * d
        acc = term if acc is None else acc + term
    o_ref[...] = acc.astype(o_ref.dtype)


def _merge(o_ret, o_swa, o_lru, w_branch, layer, z):
    m = o_ret.shape[0]
    tm, tn = 1024, 512
    gb = OFF_GATE // tn
    gstep = D_MODEL // tn
    a_spec = pl.BlockSpec((tm, BRANCH_WIDTH), lambda i, j: (i, 0))

    def gate_spec(k):
        return pl.BlockSpec((tm, tn), lambda i, j: (i, gb + k * gstep + j))

    return pl.pallas_call(
        _merge_kernel,
        grid=(m // tm, D_MODEL // tn),
        in_specs=[a_spec, a_spec, a_spec,
                  pl.BlockSpec((None, N_BRANCH, BRANCH_WIDTH, tn), lambda i, j: (layer, 0, 0, j)),
                  gate_spec(0), gate_spec(1), gate_spec(2)],
        out_specs=pl.BlockSpec((tm, tn), lambda i, j: (i, j)),
        out_shape=jax.ShapeDtypeStruct((m, D_MODEL), BF16),
        compiler_params=_params(("parallel", "parallel")),
        name="merge",
    )(o_ret, o_swa, o_lru, w_branch, z, z, z)


def kernel(x, positions, norm_mix, w_in, swa_q_gain, swa_k_gain, swa_sinks, conv_w, conv_b,
           lru_wa, lru_ba, lru_wx, lru_bx, lru_lambda, w_branch, w_out, norm_mlp,
           w_mlp_in, w_mlp_out):
    batch, seq, d = x.shape
    m = batch * seq
    xf = x.reshape(m, d)
    rc, rs, sc, ss = _rope_tables(positions.reshape(m, 1))
    w_branch, w_out, w_mlp_out = (w.astype(BF16) for w in (w_branch, w_out, w_mlp_out))
    for l in range(DEPTH):
        h = _rmsnorm(xf, norm_mix[l])
        z = _matmul(h, w_in, l, tm=2048, tn=512, single_buffer_a=True, name="in_proj")
        o_ret = _retention(z, rc, rs, batch, seq)
        o_swa = _swa(z, sc, ss, swa_q_gain[l], swa_k_gain[l], swa_sinks[l], batch, seq)
        o_lru = _rglru(z, conv_w[l], conv_b[l], lru_wa[l], lru_ba[l], lru_wx[l], lru_bx[l],
                       lru_lambda[l], batch, seq)
        mixed = _merge(o_ret, o_swa, o_lru, w_branch, l, z)
        xf = _matmul(mixed, w_out, l, epilogue="residual", residual=xf, name="out_proj")
        h = _rmsnorm(xf, norm_mlp[l])
        u = _matmul(h, w_mlp_in, l, epilogue="relu2", tm=2048, tn=512, single_buffer_a=True,
                    name="mlp_in")
        xf = _matmul(u, w_mlp_out, l, epilogue="residual", residual=xf, tk=4096,
                     name="mlp_out")
    return xf.reshape(batch, seq, d)
```

```python
import functools

import jax
import jax.numpy as jnp
from jax import lax
from jax.experimental import pallas as pl
from jax.experimental.pallas import tpu as pltpu

F32 = jnp.float32
BF16 = jnp.bfloat16

D_MODEL = 4096
DEPTH = 2
HEAD_DIM = 128
RET_HEADS = D_MODEL // 512
RET_DK = HEAD_DIM
RET_DV = 2 * HEAD_DIM
RET_THETA = 10000.0
SWA_Q_HEADS = D_MODEL // 256
SWA_KV_HEADS = SWA_Q_HEADS // 4
SWA_GROUP = SWA_Q_HEADS // SWA_KV_HEADS
SWA_WINDOW = 128
ROPE_THETA = 500000.0
ROPE_DIM = HEAD_DIM // 4
LRU_WIDTH = D_MODEL // 2
LRU_BLOCKS = 16
LRU_BLOCK_DIM = LRU_WIDTH // LRU_BLOCKS
CONV_WIDTH = 4
LRU_C = 8.0
N_BRANCH = 3
BRANCH_WIDTH = D_MODEL // 2
D_FF = 4 * D_MODEL
EPS = 1e-6

OFF_RQ = 0
OFF_RK = OFF_RQ + RET_HEADS * RET_DK
OFF_RV = OFF_RK + RET_HEADS * RET_DK
OFF_RG = OFF_RV + RET_HEADS * RET_DV
OFF_SQ = OFF_RG + RET_HEADS * RET_DV
OFF_SK = OFF_SQ + SWA_Q_HEADS * HEAD_DIM
OFF_SV = OFF_SK + SWA_KV_HEADS * HEAD_DIM
OFF_LX = OFF_SV + SWA_KV_HEADS * HEAD_DIM
OFF_LY = OFF_LX + LRU_WIDTH
OFF_GATE = OFF_LY + LRU_WIDTH
IN_TOTAL = OFF_GATE + N_BRANCH * D_MODEL

LANES = 128
SUBLANES = 8
VMEM_LIMIT = 56 * 1024 * 1024


def _params(semantics, vmem=VMEM_LIMIT):
    return pltpu.CompilerParams(dimension_semantics=semantics, vmem_limit_bytes=vmem)


def _sigmoid(x):
    return 0.5 * jnp.tanh(0.5 * x) + 0.5


def _rope_tables_kernel(pos_ref, fr_ref, sr_ref, fs_ref, rc_ref, rs_ref, sc_ref, ss_ref):
    pos = pos_ref[...].astype(F32)
    ang_r = pos * fr_ref[...]
    rc_ref[...] = jnp.cos(ang_r)
    rs_ref[...] = jnp.sin(ang_r) * sr_ref[...]
    ang_s = pos * fs_ref[...]
    sc_ref[...] = jnp.cos(ang_s)
    ss_ref[...] = jnp.sin(ang_s)


def _rope_tables(pos):
    m = pos.shape[0]
    t = 1024
    half_r = RET_DK // 2
    f_r = RET_THETA ** (-jnp.arange(half_r, dtype=F32) / half_r)
    fr = jnp.concatenate([f_r, f_r])[None, :]
    sr = jnp.concatenate([-jnp.ones((half_r,), F32), jnp.ones((half_r,), F32)])[None, :]
    half_s = ROPE_DIM // 2
    f_s = ROPE_THETA ** (-jnp.arange(half_s, dtype=F32) / half_s)
    zeros_rest = jnp.zeros((HEAD_DIM - ROPE_DIM,), F32)
    fs = jnp.concatenate([f_s, f_s, zeros_rest])[None, :]
    row = pl.BlockSpec((1, LANES), lambda i: (0, 0))
    tab = pl.BlockSpec((t, LANES), lambda i: (i, 0))
    return pl.pallas_call(
        _rope_tables_kernel,
        grid=(m // t,),
        in_specs=[pl.BlockSpec((t, 1), lambda i: (i, 0)), row, row, row],
        out_specs=[tab] * 4,
        out_shape=[jax.ShapeDtypeStruct((m, LANES), F32)] * 4,
        compiler_params=_params(("parallel",)),
        name="rope_tables",
    )(pos, fr, sr, fs)


def _rmsnorm_kernel(x_ref, g_ref, o_ref):
    x = x_ref[...]
    ms = jnp.mean(x * x, axis=-1, keepdims=True)
    o_ref[...] = (x * lax.rsqrt(ms + EPS) * g_ref[...]).astype(o_ref.dtype)


def _rmsnorm(x, gain):
    m, d = x.shape
    t = 512
    return pl.pallas_call(
        _rmsnorm_kernel,
        grid=(m // t,),
        in_specs=[pl.BlockSpec((t, d), lambda i: (i, 0)),
                  pl.BlockSpec((1, d), lambda i: (0, 0))],
        out_specs=pl.BlockSpec((t, d), lambda i: (i, 0)),
        out_shape=jax.ShapeDtypeStruct((m, d), BF16),
        compiler_params=_params(("parallel",)),
        name="rmsnorm",
    )(x, gain[None, :])


def _mm_kernel(a_ref, b_ref, *rest, nk, epilogue, side_slots):
    ns = len(side_slots)
    if ns:
        side_in = rest[len(rest) - 2 * ns - 1:len(rest) - ns - 1]
        side_out = rest[len(rest) - ns:]
        rest = rest[:len(rest) - 2 * ns - 1] + (rest[len(rest) - ns - 1],)
        for src_ref, dst_ref in zip(side_in, side_out):
            dst_ref[...] = src_ref[...].astype(dst_ref.dtype)

    if epilogue == "residual":
        r_ref, o_ref = rest
        if nk == 1:
            o_ref[...] = r_ref[...] + jnp.dot(a_ref[...], b_ref[...].astype(BF16), preferred_element_type=F32)
        else:
            @pl.when(pl.program_id(2) == 0)
            def _():
                o_ref[...] = r_ref[...]

            o_ref[...] += jnp.dot(a_ref[...], b_ref[...].astype(BF16), preferred_element_type=F32)
    else:
        (o_ref,) = rest
        d = jnp.dot(a_ref[...], b_ref[...].astype(BF16), preferred_element_type=F32)
        if epilogue == "relu2":
            d = jnp.maximum(d, 0.0)
            d = d * d
        o_ref[...] = d.astype(o_ref.dtype)


def _matmul(a, b, layer, *, epilogue="cast", residual=None, tm=1024, tn=1024, tk=None,
            single_buffer_a=False, side_casts=(), name="matmul"):
    m, kdim = a.shape
    n = b.shape[2]
    tk = kdim if tk is None else tk
    nk = kdim // tk
    ni, nj = m // tm, n // tn
    assert m % tm == 0 and n % tn == 0 and kdim % tk == 0
    assert nk == 1 or epilogue == "residual"
    a_mode = dict(pipeline_mode=pl.Buffered(1)) if single_buffer_a else {}
    in_specs = [pl.BlockSpec((tm, tk), lambda i, j, k: (i, k), **a_mode),
                pl.BlockSpec((None, tk, tn), lambda i, j, k: (layer, k, j))]
    args = [a, b]
    if epilogue == "residual":
        in_specs.append(pl.BlockSpec((tm, tn), lambda i, j, k: (i, j)))
        args.append(residual)
        out_dtype = F32
    else:
        out_dtype = BF16
    out_specs = [pl.BlockSpec((tm, tn), lambda i, j, k: (i, j))]
    out_shape = [jax.ShapeDtypeStruct((m, n), out_dtype)]
    side_slots = []
    for src, njs in side_casts:
        assert nk == 1 and njs <= nj
        _, rows, cols = src.shape
        slab = rows // (ni * njs)
        assert slab * ni * njs == rows and slab % 16 == 0

        def slot(i, j, k, njs=njs):
            return i * njs + jnp.minimum(j, njs - 1)

        in_specs.append(pl.BlockSpec((None, slab, cols),
                                     lambda i, j, k, slot=slot: (layer, slot(i, j, k), 0)))
        args.append(src)
        out_specs.append(pl.BlockSpec((slab, cols), lambda i, j, k, slot=slot: (slot(i, j, k), 0)))
        out_shape.append(jax.ShapeDtypeStruct((rows, cols), BF16))
        side_slots.append((njs, nj))
    outs = pl.pallas_call(
        functools.partial(_mm_kernel, nk=nk, epilogue=epilogue, side_slots=tuple(side_slots)),
        grid=(ni, nj, nk),
        in_specs=in_specs,
        out_specs=out_specs,
        out_shape=out_shape,
        compiler_params=_params(("parallel",
                                 "arbitrary" if any(s < t for s, t in side_slots) else "parallel",
                                 "arbitrary")),
        name=name,
    )(*args)
    return outs if side_casts else outs[0]


def _ret_kernel(q_ref, k_ref, v_ref, g_ref, rc_ref, rs_ref, lg_ref, o_ref,
                state_ref, dmat_ref, *, t, hp):
    n = pl.program_id(2)
    ii = lax.broadcasted_iota(jnp.int32, (t, 1), 0).astype(F32)
    lgs = [lg_ref[i][:, :1] for i in range(hp)]

    @pl.when(n == 0)
    def _():
        state_ref[...] = jnp.zeros_like(state_ref)
        r = lax.broadcasted_iota(jnp.int32, (t, t), 0)
        c = lax.broadcasted_iota(jnp.int32, (t, t), 1)
        diff = (r - c).astype(F32)
        for i in range(hp):
            dmat_ref[i] = jnp.where(diff >= 0.0, jnp.exp(lgs[i] * jnp.maximum(diff, 0.0)), 0.0)

    rc = rc_ref[...]
    rs = rs_ref[...]
    for i in range(hp):
        lg = lgs[i]
        q = q_ref[:, i * RET_DK:(i + 1) * RET_DK].astype(F32)
        k = k_ref[:, i * RET_DK:(i + 1) * RET_DK].astype(F32)
        qr = q * rc + pltpu.roll(q, RET_DK // 2, 1) * rs
        kr = (k * rc + pltpu.roll(k, RET_DK // 2, 1) * rs) * (RET_DK ** -0.5)
        v = v_ref[:, i * RET_DV:(i + 1) * RET_DV]
        s = lax.dot_general(qr.astype(BF16), kr.astype(BF16), (((1,), (1,)), ((), ())),
                            preferred_element_type=F32)
        p = (s * dmat_ref[i]).astype(BF16)
        intra = jnp.dot(p, v, preferred_element_type=F32)
        xi = jnp.exp(lg * (ii + 1.0))
        st = state_ref[i]
        cross = jnp.dot((qr * xi).astype(BF16), st.astype(BF16), preferred_element_type=F32)
        zeta = jnp.exp(lg * (t - 1.0 - ii))
        kz = (kr * zeta).astype(BF16)
        kv = lax.dot_general(kz, v, (((0,), (0,)), ((), ())), preferred_element_type=F32)
        state_ref[i] = jnp.exp(lg * float(t)) * st + kv
        y = intra + cross
        y = y * lax.rsqrt(jnp.mean(y * y, axis=-1, keepdims=True) + EPS)
        g = g_ref[:, i * RET_DV:(i + 1) * RET_DV].astype(F32)
        o_ref[:, i * RET_DV:(i + 1) * RET_DV] = (g * _sigmoid(g) * y).astype(o_ref.dtype)


def _retention(z, rc, rs, batch, seq):
    t = 512
    hp = 4
    nb = seq // t
    log_g = jnp.log1p(-jnp.exp2(-5.0 - jnp.arange(RET_HEADS, dtype=F32)))
    lg = jnp.broadcast_to(log_g[:, None, None], (RET_HEADS, 1, LANES))
    kb = OFF_RK // (hp * RET_DK)
    vb = OFF_RV // (hp * RET_DV)
    gb = OFF_RG // (hp * RET_DV)
    return pl.pallas_call(
        functools.partial(_ret_kernel, t=t, hp=hp),
        grid=(batch, RET_HEADS // hp, nb),
        in_specs=[
            pl.BlockSpec((t, hp * RET_DK), lambda b, h, n: (b * nb + n, h)),
            pl.BlockSpec((t, hp * RET_DK), lambda b, h, n: (b * nb + n, kb + h)),
            pl.BlockSpec((t, hp * RET_DV), lambda b, h, n: (b * nb + n, vb + h)),
            pl.BlockSpec((t, hp * RET_DV), lambda b, h, n: (b * nb + n, gb + h)),
            pl.BlockSpec((t, LANES), lambda b, h, n: (b * nb + n, 0)),
            pl.BlockSpec((t, LANES), lambda b, h, n: (b * nb + n, 0)),
            pl.BlockSpec((hp, 1, LANES), lambda b, h, n: (h, 0, 0)),
        ],
        out_specs=pl.BlockSpec((t, hp * RET_DV), lambda b, h, n: (b * nb + n, h)),
        out_shape=jax.ShapeDtypeStruct((batch * seq, RET_HEADS * RET_DV), BF16),
        scratch_shapes=[pltpu.VMEM((hp, RET_DK, RET_DV), F32), pltpu.VMEM((hp, t, t), F32)],
        compiler_params=_params(("parallel", "parallel", "arbitrary")),
        name="retention",
    )(z, z, z, z, rc, rs, lg)


def _norm_rope(x, gain, c, s, ones, rot):
    xf = x.astype(F32)
    ss = jnp.dot((xf * xf).astype(BF16), ones, preferred_element_type=F32)
    xn = xf * lax.rsqrt(ss * (1.0 / HEAD_DIM) + EPS) * gain
    partner = jnp.dot(xn.astype(BF16), rot, preferred_element_type=F32)
    return xn * c + partner * s


def _swa_kernel(q_ref, kc_ref, kp_ref, vc_ref, vp_ref, c_ref, s_ref, cp_ref, sp_ref,
                qg_ref, kg_ref, sink_ref, rot_ref, o_ref, *, tq):
    n = pl.program_id(2)
    w = SWA_WINDOW
    group_rows = SWA_GROUP * w
    ones = jnp.ones((HEAD_DIM, HEAD_DIM), BF16)
    rot = rot_ref[...]
    qg = qg_ref[...] * (HEAD_DIM ** -0.5)
    kg = kg_ref[...]
    c = c_ref[...]
    s = s_ref[...]
    k_prev = _norm_rope(kp_ref[...], kg, cp_ref[...], sp_ref[...], ones, rot)
    k_cur = _norm_rope(kc_ref[...], kg, c, s, ones, rot)
    k_all = jnp.concatenate([k_prev, k_cur], axis=0).astype(BF16)
    v_all = jnp.concatenate([vp_ref[...], vc_ref[...]], axis=0)
    v_t = v_all.astype(F32).T.astype(BF16)
    ones_t = jnp.ones((HEAD_DIM, 2 * w), BF16)
    sink = sink_ref[0]
    kj = lax.broadcasted_iota(jnp.int32, (2 * w, w), 0)
    qi = lax.broadcasted_iota(jnp.int32, (2 * w, w), 1)
    band = (kj > qi) & (kj <= qi + w)
    bias = jnp.concatenate([jnp.where(band, 0.0, -jnp.inf)] * SWA_GROUP, axis=1)
    bias_first = jnp.concatenate(
        [jnp.where(band & (kj >= jnp.where(n > 0, 0, w)), 0.0, -jnp.inf)] * SWA_GROUP, axis=1)
    nt = tq // w
    xq = jnp.concatenate([q_ref[t * w:(t + 1) * w, g * HEAD_DIM:(g + 1) * HEAD_DIM]
                          for t in range(nt) for g in range(SWA_GROUP)], axis=0)
    c4 = jnp.concatenate([c[t * w:(t + 1) * w] for t in range(nt)
                          for _ in range(SWA_GROUP)], axis=0)
    s4 = jnp.concatenate([s[t * w:(t + 1) * w] for t in range(nt)
                          for _ in range(SWA_GROUP)], axis=0)
    qs = _norm_rope(xq, qg, c4, s4, ones, rot).astype(BF16)
    sc = [lax.dot_general(k_all[t * w:(t + 2) * w], qs[t * group_rows:(t + 1) * group_rows],
                          (((1,), (1,)), ((), ())), preferred_element_type=F32)
          + (bias_first if t == 0 else bias) for t in range(nt)]
    m = [jnp.maximum(jnp.max(x, axis=0, keepdims=True), sink) for x in sc]
    p = [jnp.exp(x - mx).astype(BF16) for x, mx in zip(sc, m)]
    o_ext = [jnp.dot(jnp.concatenate([v_t[:, t * w:(t + 2) * w], ones_t], axis=0), p[t],
                     preferred_element_type=F32) for t in range(nt)]
    for t in range(nt):
        denom = o_ext[t][HEAD_DIM:HEAD_DIM + 1] + jnp.exp(sink - m[t])
        o_t = o_ext[t][:HEAD_DIM] / denom
        for g in range(SWA_GROUP):
            o_ref[t * w:(t + 1) * w, g * HEAD_DIM:(g + 1) * HEAD_DIM] = (
                o_t[:, g * w:(g + 1) * w].T.astype(o_ref.dtype))


def _rotate_half_matrix():
    half = ROPE_DIM // 2
    src = jnp.arange(HEAD_DIM)[:, None]
    dst = jnp.arange(HEAD_DIM)[None, :]
    minus = (dst < half) & (src == dst + half)
    plus = (dst >= half) & (dst < ROPE_DIM) & (src == dst - half)
    return (plus.astype(F32) - minus.astype(F32)).astype(BF16)


def _swa(z, sc, ss, q_gain, k_gain, sinks, batch, seq):
    tq = 1024
    w = SWA_WINDOW
    nb = seq // tq
    r = tq // w
    gw = SWA_GROUP * HEAD_DIM
    qb = OFF_SQ // gw
    kb = OFF_SK // HEAD_DIM
    vb = OFF_SV // HEAD_DIM
    sink_b = jnp.repeat(sinks.astype(F32).reshape(SWA_KV_HEADS, SWA_GROUP), w,
                        axis=1)[:, None, :]

    def cur(b, h, n):
        return b * nb + n

    def prev(b, h, n):
        return b * nb * r + jnp.maximum(n * r - 1, 0)

    return pl.pallas_call(
        functools.partial(_swa_kernel, tq=tq),
        grid=(batch, SWA_KV_HEADS, nb),
        in_specs=[
            pl.BlockSpec((tq, gw), lambda b, h, n: (cur(b, h, n), qb + h)),
            pl.BlockSpec((tq, HEAD_DIM), lambda b, h, n: (cur(b, h, n), kb + h)),
            pl.BlockSpec((w, HEAD_DIM), lambda b, h, n: (prev(b, h, n), kb + h)),
            pl.BlockSpec((tq, HEAD_DIM), lambda b, h, n: (cur(b, h, n), vb + h)),
            pl.BlockSpec((w, HEAD_DIM), lambda b, h, n: (prev(b, h, n), vb + h)),
            pl.BlockSpec((tq, LANES), lambda b, h, n: (cur(b, h, n), 0)),
            pl.BlockSpec((tq, LANES), lambda b, h, n: (cur(b, h, n), 0)),
            pl.BlockSpec((w, LANES), lambda b, h, n: (prev(b, h, n), 0)),
            pl.BlockSpec((w, LANES), lambda b, h, n: (prev(b, h, n), 0)),
            pl.BlockSpec((1, HEAD_DIM), lambda b, h, n: (0, 0)),
            pl.BlockSpec((1, HEAD_DIM), lambda b, h, n: (0, 0)),
            pl.BlockSpec((1, 1, SWA_GROUP * w), lambda b, h, n: (h, 0, 0)),
            pl.BlockSpec((HEAD_DIM, HEAD_DIM), lambda b, h, n: (0, 0)),
        ],
        out_specs=pl.BlockSpec((tq, gw), lambda b, h, n: (cur(b, h, n), h)),
        out_shape=jax.ShapeDtypeStruct((batch * seq, SWA_Q_HEADS * HEAD_DIM), BF16),
        compiler_params=_params(("parallel", "parallel", "parallel")),
        name="swa",
    )(z, z, z, z, z, sc, ss, sc, ss, q_gain[None, :], k_gain[None, :], sink_b,
      _rotate_half_matrix())


def _lru_kernel(x_ref, y_ref, cw_ref, cb_ref, w_ref, b_ref, lam_ref, o_ref,
                xs_ref, a_ref, u_ref, carry_ref, *, t, nblk):
    n = pl.program_id(2)
    pad = SUBLANES
    nseg = SUBLANES
    seg = t // nseg
    pitch = _segment_pitch(seg)
    wd = LRU_BLOCK_DIM

    @pl.when(n == 0)
    def _():
        xs_ref[:, 0:pad, :] = jnp.zeros((nblk, pad, wd), F32)
        carry_ref[...] = jnp.zeros_like(carry_ref)

    for blk in range(nblk):
        lanes = slice(blk * wd, (blk + 1) * wd)
        xs = xs_ref.at[blk]
        xs[pad:pad + t, :] = x_ref[:, lanes].astype(F32)
        cw = cw_ref[:, lanes]
        xc = cb_ref[:, lanes] + xs[pad:pad + t, :] * cw[CONV_WIDTH - 1:CONV_WIDTH]
        for d in range(1, CONV_WIDTH):
            xc = xc + xs[pl.ds(pad - d, t), :] * cw[CONV_WIDTH - 1 - d:CONV_WIDTH - d]
        xs[0:pad, :] = xs[t:t + pad, :]

        half_gates = (jnp.dot(xc.astype(BF16), w_ref[blk], preferred_element_type=F32)
                      + b_ref[blk])
        tanh_r = jnp.tanh(half_gates[:, :wd])
        i = 0.5 * jnp.tanh(half_gates[:, wd:]) + 0.5
        half_rate = (-0.5 * LRU_C) * jax.nn.softplus(-lam_ref[:, lanes])
        a = jnp.exp(half_rate * tanh_r + half_rate)
        w = 1.0 - a * a
        u = w * lax.rsqrt(jnp.maximum(w, 1e-30)) * i * xc
        for s in range(nseg):
            a_ref[blk, s * pitch:s * pitch + seg, :] = a[s * seg:(s + 1) * seg]
            u_ref[blk, s * pitch:s * pitch + seg, :] = u[s * seg:(s + 1) * seg]

    hs = [jnp.zeros((nseg, wd), F32)] * nblk
    ps = [jnp.ones((nseg, wd), F32)] * nblk
    for j in range(seg):
        rows = pl.ds(j, nseg, stride=pitch)
        for blk in range(nblk):
            aj = a_ref[blk, rows, :]
            hs[blk] = aj * hs[blk] + u_ref[blk, rows, :]
            ps[blk] = aj * ps[blk]
            u_ref[blk, rows, :] = hs[blk]
            a_ref[blk, rows, :] = ps[blk]

    for blk in range(nblk):
        lanes = slice(blk * wd, (blk + 1) * wd)
        c = carry_ref[blk]
        h = []
        for s in range(nseg):
            h.append(u_ref[blk, s * pitch:s * pitch + seg, :]
                     + a_ref[blk, s * pitch:s * pitch + seg, :] * c)
            c = ps[blk][s:s + 1] * c + hs[blk][s:s + 1]
        carry_ref[blk] = c
        h = jnp.concatenate(h, axis=0)
        o_ref[:, lanes] = (h * jax.nn.gelu(y_ref[:, lanes].astype(F32))).astype(o_ref.dtype)


def _segment_pitch(seg):
    pitch = -(-seg // 4) * 4
    while (pitch // 4) % 2 == 0:
        pitch += 4
    return pitch


def _rglru(z, conv_w, conv_b, wa, ba, wx, bx, lam, batch, seq):
    t = 2048
    nb = seq // t
    cwid = LRU_BLOCK_DIM
    scan_rows = SUBLANES * _segment_pitch(t // SUBLANES)
    xb = OFF_LX // cwid
    yb = OFF_LY // cwid
    w_cat = (0.5 * jnp.concatenate([wa, wx], axis=-1)).astype(BF16)
    b_cat = 0.5 * jnp.concatenate([ba.reshape(LRU_BLOCKS, 1, cwid),
                                   bx.reshape(LRU_BLOCKS, 1, cwid)], axis=-1)
    nblk = 2
    cw = nblk * cwid
    xb //= nblk
    yb //= nblk
    return pl.pallas_call(
        functools.partial(_lru_kernel, t=t, nblk=nblk),
        grid=(batch, LRU_BLOCKS // nblk, nb),
        in_specs=[
            pl.BlockSpec((t, cw), lambda b, c, n: (b * nb + n, xb + c)),
            pl.BlockSpec((t, cw), lambda b, c, n: (b * nb + n, yb + c)),
            pl.BlockSpec((CONV_WIDTH, cw), lambda b, c, n: (0, c)),
            pl.BlockSpec((1, cw), lambda b, c, n: (0, c)),
            pl.BlockSpec((nblk, cwid, 2 * cwid), lambda b, c, n: (c, 0, 0)),
            pl.BlockSpec((nblk, 1, 2 * cwid), lambda b, c, n: (c, 0, 0)),
            pl.BlockSpec((1, cw), lambda b, c, n: (0, c)),
        ],
        out_specs=pl.BlockSpec((t, cw), lambda b, c, n: (b * nb + n, c)),
        out_shape=jax.ShapeDtypeStruct((batch * seq, LRU_WIDTH), BF16),
        scratch_shapes=[pltpu.VMEM((nblk, SUBLANES + t, cwid), F32),
                        pltpu.VMEM((nblk, scan_rows, cwid), F32),
                        pltpu.VMEM((nblk, scan_rows, cwid), F32),
                        pltpu.VMEM((nblk, 1, cwid), F32)],
        compiler_params=_params(("parallel", "parallel", "arbitrary")),
        name="rglru",
    )(z, z, conv_w, conv_b[None, :], w_cat, b_cat, lam[None, :])


def _merge_kernel(a0_ref, a1_ref, a2_ref, w_ref, g0_ref, g1_ref, g2_ref, o_ref):
    acc = None
    for k, (a_ref, g_ref) in enumerate(((a0_ref, g0_ref), (a1_ref, g1_ref), (a2_ref, g2_ref))):
        d = jnp.dot(a_ref[...], w_ref[k], preferred_element_type=F32)
        term = _sigmoid(g_ref[...].astype(F32)) * d
        acc = term if acc is None else acc + term
    o_ref[...] = acc.astype(o_ref.dtype)


def _merge(o_ret, o_swa, o_lru, w_branch, layer, z):
    m = o_ret.shape[0]
    tm, tn = 1024, 512
    gb = OFF_GATE // tn
    gstep = D_MODEL // tn
    a_spec = pl.BlockSpec((tm, BRANCH_WIDTH), lambda i, j: (i, 0))

    def gate_spec(k):
        return pl.BlockSpec((tm, tn), lambda i, j: (i, gb + k * gstep + j))

    return pl.pallas_call(
        _merge_kernel,
        grid=(m // tm, D_MODEL // tn),
        in_specs=[a_spec, a_spec, a_spec,
                  pl.BlockSpec((None, N_BRANCH, BRANCH_WIDTH, tn), lambda i, j: (layer, 0, 0, j)),
                  gate_spec(0), gate_spec(1), gate_spec(2)],
        out_specs=pl.BlockSpec((tm, tn), lambda i, j: (i, j)),
        out_shape=jax.ShapeDtypeStruct((m, D_MODEL), BF16),
        compiler_params=_params(("parallel", "parallel")),
        name="merge",
    )(o_ret, o_swa, o_lru, w_branch, z, z, z)


def kernel(x, positions, norm_mix, w_in, swa_q_gain, swa_k_gain, swa_sinks, conv_w, conv_b,
           lru_wa, lru_ba, lru_wx, lru_bx, lru_lambda, w_branch, w_out, norm_mlp,
           w_mlp_in, w_mlp_out):
    batch, seq, d = x.shape
    m = batch * seq
    xf = x.reshape(m, d)
    rc, rs, sc, ss = _rope_tables(positions.reshape(m, 1))
    w_branch_rows = w_branch.reshape(DEPTH, N_BRANCH * BRANCH_WIDTH, D_MODEL)
    for l in range(DEPTH):
        h = _rmsnorm(xf, norm_mix[l])
        z, wb, wo = _matmul(h, w_in, l, tm=2048, tn=512, single_buffer_a=True,
                            side_casts=((w_branch_rows, 48), (w_out, 32)), name="in_proj")
        o_ret = _retention(z, rc, rs, batch, seq)
        o_swa = _swa(z, sc, ss, swa_q_gain[l], swa_k_gain[l], swa_sinks[l], batch, seq)
        o_lru = _rglru(z, conv_w[l], conv_b[l], lru_wa[l], lru_ba[l], lru_wx[l], lru_bx[l],
                       lru_lambda[l], batch, seq)
        mixed = _merge(o_ret, o_swa, o_lru,
                       wb.reshape(1, N_BRANCH, BRANCH_WIDTH, D_MODEL), 0, z)
        xf = _matmul(mixed, wo[None], 0, epilogue="residual", residual=xf, name="out_proj")
        h = _rmsnorm(xf, norm_mlp[l])
        u, w2 = _matmul(h, w_mlp_in, l, epilogue="relu2", tm=2048, tn=512, single_buffer_a=True,
                        side_casts=((w_mlp_out, 32),), name="mlp_in")
        xf = _matmul(u, w2[None], 0, epilogue="residual", residual=xf, tk=4096,
                     name="mlp_out")
    return xf.reshape(batch, seq, d)
```
